```python
import math
import jax
import jax.numpy as jnp
from jax import lax
import numpy as np

D_MODEL = 1024
BATCH = 1
SEQ = 16384
DEPTH = 2
DEC_BATCH = 32
DEC_SEQ = 4
PAST_LEN = 16384
PAGE_SIZE = 128

N_HEADS_TOTAL = 16
HEAD_DIM = D_MODEL // N_HEADS_TOTAL
H_FOX = 4
H_GDN = 4
H_RET = 4
H_DIFF = 4
W_FOX = H_FOX * HEAD_DIM
W_GDN = H_GDN * HEAD_DIM
W_RET = H_RET * HEAD_DIM
W_DIFF = H_DIFF * HEAD_DIM
D_MIX = W_FOX + W_GDN + W_RET + W_DIFF
RET_QK = HEAD_DIM // 2
DIFF_QK = HEAD_DIM // 2
GDN_CONV = 4
GDN_CHUNK = 64
RET_CHUNK = 128
Q_BLOCK = 128
N_GROUPS = 4
E_PER_GROUP = 4
N_EXPERTS = N_GROUPS * E_PER_GROUP
TOP_K = 2
D_FF_EXPERT = 256
RMS_EPS = 1e-6
POOL_NUM = 5
POOL_DEN = 4
FOX_BIAS_INIT = 2.5
SPLIT_SIZES = (
    W_FOX, W_FOX, W_FOX, H_FOX,
    W_GDN, W_GDN, W_GDN, H_GDN, H_GDN, W_GDN,
    H_RET * RET_QK, H_RET * RET_QK, W_RET, W_RET,
    H_DIFF * 2 * DIFF_QK, H_DIFF * 2 * DIFF_QK, W_DIFF,
)
D_IN = sum(SPLIT_SIZES)

kernel_name = 'hybrid_fox_gdn_ret_diff_hmoe_step'

F32 = jnp.float32


def rmsnorm(x, g):
    xf = x.astype(F32)
    y = xf * lax.rsqrt(jnp.mean(xf * xf, axis=-1, keepdims=True) + RMS_EPS)
    return (y * g.astype(F32)).astype(x.dtype)


def l2norm(x):
    return x * lax.rsqrt(jnp.sum(x * x, axis=-1, keepdims=True) + 1e-6)


def heads(x, n):
    return x.reshape(*x.shape[:-1], n, x.shape[-1] // n)


def split_columns(z):
    idx = np.cumsum(SPLIT_SIZES)[:-1].tolist()
    return jnp.split(z, idx, axis=-1)


def alibi_slopes(n):
    return jnp.exp2(-8.0 * jnp.arange(1, n + 1, dtype=F32) / n)


def retention_log_decay():
    return jnp.log(1.0 - jnp.exp2(-5.0 - jnp.arange(H_RET, dtype=F32)))


def gather_pages(pool, page_table):
    g = jnp.take(pool, page_table, axis=0)
    return g.reshape(g.shape[0], g.shape[1] * g.shape[2], *g.shape[3:])


def softmax_two(s_past, s_new):
    p = jax.nn.softmax(jnp.concatenate([s_past, s_new], axis=-1), axis=-1)
    n_past = s_past.shape[-1]
    return p[..., :n_past], p[..., n_past:]


def to_chunks(x, c):
    b, l, h = x.shape[:3]
    y = jnp.moveaxis(x, 2, 1).reshape(b, h, l // c, c, *x.shape[3:])
    return jnp.moveaxis(y, 2, 0)


def from_chunks(o):
    n, b, h, c = o.shape[:4]
    y = jnp.moveaxis(o, 0, 2).reshape(b, h, n * c, *o.shape[4:])
    return jnp.moveaxis(y, 1, 2)


def fox_prompt(q, k, v, logf):
    B, S, H, dh = q.shape
    nb = S // Q_BLOCK
    scale = dh ** -0.5
    c = jnp.cumsum(logf, axis=1).transpose(0, 2, 1)
    qb = q.reshape(B, nb, Q_BLOCK, H, dh).swapaxes(0, 1)
    cb = c.reshape(B, H, nb, Q_BLOCK).transpose(2, 0, 1, 3)
    kpos = jnp.arange(S)

    def block(args):
        i, qi, ci = args
        s = jnp.einsum('bqhd,bkhd->bhqk', qi, k, preferred_element_type=F32) * scale
        s = s + ci[..., :, None] - c[..., None, :]
        qpos = i * Q_BLOCK + jnp.arange(Q_BLOCK)
        s = jnp.where(kpos[None, :] <= qpos[:, None], s, -jnp.inf)
        p = jax.nn.softmax(s, axis=-1).astype(v.dtype)
        return jnp.einsum('bhqk,bkhd->bqhd', p, v)

    o = lax.map(block, (jnp.arange(nb), qb, cb))
    return o.swapaxes(0, 1).reshape(B, S, H, dh)


def fox_sample(q, k, v, logf, k_past, v_past, logf_past):
    L = q.shape[1]
    scale = q.shape[-1] ** -0.5
    c_past = jnp.cumsum(logf_past.astype(F32), axis=1)
    c_new = c_past[:, -1:] + jnp.cumsum(logf, axis=1)
    cp = c_past.transpose(0, 2, 1)
    cn = c_new.transpose(0, 2, 1)
    s_past = jnp.einsum('nqhd,nkhd->nhqk', q, k_past, preferred_element_type=F32) * scale
    s_past = s_past + cn[..., :, None] - cp[..., None, :]
    s_new = jnp.einsum('nqhd,nkhd->nhqk', q, k, preferred_element_type=F32) * scale
    s_new = s_new + cn[..., :, None] - cn[..., None, :]
    s_new = jnp.where(jnp.tril(jnp.ones((L, L), bool)), s_new, -jnp.inf)
    p_past, p_new = softmax_two(s_past, s_new)
    return (jnp.einsum('nhqk,nkhd->nqhd', p_past.astype(v.dtype), v_past)
            + jnp.einsum('nhqk,nkhd->nqhd', p_new.astype(v.dtype), v))


def diff_prompt(q, k, v, lam, slopes):
    B, S, H, dq = q.shape
    nb = S // Q_BLOCK
    scale = DIFF_QK ** -0.5
    k1, k2 = k[..., :DIFF_QK], k[..., DIFF_QK:]
    qb = q.reshape(B, nb, Q_BLOCK, H, dq).swapaxes(0, 1)
    kpos = jnp.arange(S)

    def block(args):
        i, qi = args
        qpos = i * Q_BLOCK + jnp.arange(Q_BLOCK)
        dist = (qpos[:, None] - kpos[None, :]).astype(F32)
        bias = jnp.where(dist >= 0, -slopes[:, None, None] * dist, -jnp.inf)
        s1 = jnp.einsum('bqhd,bkhd->bhqk', qi[..., :DIFF_QK], k1, preferred_element_type=F32) * scale + bias
        s2 = jnp.einsum('bqhd,bkhd->bhqk', qi[..., DIFF_QK:], k2, preferred_element_type=F32) * scale + bias
        w = jax.nn.softmax(s1, axis=-1) - lam * jax.nn.softmax(s2, axis=-1)
        return jnp.einsum('bhqk,bkhd->bqhd', w.astype(v.dtype), v)

    o = lax.map(block, (jnp.arange(nb), qb))
    return o.swapaxes(0, 1).reshape(B, S, H, v.shape[-1])


def diff_sample(q, k, v, lam, slopes, k_past, v_past):
    L = q.shape[1]
    P = k_past.shape[1]
    scale = DIFF_QK ** -0.5
    qpos = P + jnp.arange(L)
    d_past = (qpos[:, None] - jnp.arange(P)[None, :]).astype(F32)
    d_new = (qpos[:, None] - qpos[None, :]).astype(F32)
    b_past = -slopes[:, None, None] * d_past
    b_new = jnp.where(d_new >= 0, -slopes[:, None, None] * d_new, -jnp.inf)

    def probs(qa, kpa, kna):
        sp = jnp.einsum('nqhd,nkhd->nhqk', qa, kpa, preferred_element_type=F32) * scale + b_past
        sn = jnp.einsum('nqhd,nkhd->nhqk', qa, kna, preferred_element_type=F32) * scale + b_new
        return softmax_two(sp, sn)

    p1p, p1n = probs(q[..., :DIFF_QK], k_past[..., :DIFF_QK], k[..., :DIFF_QK])
    p2p, p2n = probs(q[..., DIFF_QK:], k_past[..., DIFF_QK:], k[..., DIFF_QK:])
    w_past = (p1p - lam * p2p).astype(v.dtype)
    w_new = (p1n - lam * p2n).astype(v.dtype)
    return (jnp.einsum('nhqk,nkhd->nqhd', w_past, v_past)
            + jnp.einsum('nhqk,nkhd->nqhd', w_new, v))


def short_conv(x, prev, w):
    L = x.shape[1]
    xx = jnp.concatenate([prev.astype(x.dtype), x], axis=1)
    y = xx[:, 0:L] * w[0]
    for j in range(1, GDN_CONV):
        y = y + xx[:, j:j + L] * w[j]
    return jax.nn.silu(y), xx[:, xx.shape[1] - (GDN_CONV - 1):]


def gated_delta_rule(q, k, v, g, beta, S0):
    L = q.shape[1]
    dv = v.shape[-1]
    C = GDN_CHUNK if L % GDN_CHUNK == 0 else L
    incl = jnp.tril(jnp.ones((C, C), bool))
    strict = jnp.tril(jnp.ones((C, C), bool), -1)
    eye = jnp.eye(C, dtype=F32)

    def step(S, inp):
        qc, kc, vc, gc, bc = inp
        G = jnp.cumsum(gc, axis=-1)
        rel = G[..., :, None] - G[..., None, :]
        decay = jnp.where(incl, jnp.exp(jnp.where(incl, rel, 0.0)), 0.0)
        kk = jnp.einsum('bhid,bhjd->bhij', kc, kc)
        a = jnp.where(strict, bc[..., :, None] * kk * decay, 0.0)
        rhs = jnp.concatenate([vc * bc[..., None], kc * (bc * jnp.exp(G))[..., None]], axis=-1)
        sol = lax.linalg.triangular_solve(eye + a, rhs, left_side=True, lower=True)
        u = sol[..., :dv] - jnp.einsum('bhik,bhkv->bhiv', sol[..., dv:], S)
        qk = jnp.einsum('bhid,bhjd->bhij', qc, kc) * decay
        o = (jnp.einsum('bhid,bhdv->bhiv', qc * jnp.exp(G)[..., None], S)
             + jnp.einsum('bhij,bhjv->bhiv', qk, u))
        g_last = G[..., -1:]
        S = (S * jnp.exp(g_last)[..., None]
             + jnp.einsum('bhid,bhiv->bhdv', kc * jnp.exp(g_last - G)[..., None], u))
        return S, o

    S, o = lax.scan(step, S0, (to_chunks(q, C), to_chunks(k, C), to_chunks(v, C),
                               to_chunks(g, C), to_chunks(beta, C)))
    return from_chunks(o), S


def retention(q, k, v, S0, log_gamma):
    L = q.shape[1]
    C = RET_CHUNK if L % RET_CHUNK == 0 else L
    idx = jnp.arange(C, dtype=F32)
    rel = idx[:, None] - idx[None, :]
    D = jnp.where(rel >= 0, jnp.exp(jnp.maximum(rel, 0.0) * log_gamma[:, None, None]), 0.0)
    q_dec = jnp.exp((idx + 1.0)[None, :] * log_gamma[:, None])
    k_dec = jnp.exp((C - 1.0 - idx)[None, :] * log_gamma[:, None])
    c_dec = jnp.exp(C * log_gamma)[:, None, None]

    def step(S, inp):
        qc, kc, vc = inp
        inner = jnp.einsum('bhij,bhjv->bhiv', jnp.einsum('bhid,bhjd->bhij', qc, kc) * D, vc)
        o = inner + jnp.einsum('bhid,bhdv->bhiv', qc * q_dec[..., None], S)
        S = S * c_dec + jnp.einsum('bhjd,bhjv->bhdv', kc * k_dec[..., None], vc)
        return S, o

    S, o = lax.scan(step, S0, (to_chunks(q, C), to_chunks(k, C), to_chunks(v, C)))
    return from_chunks(o), S


def token_mix(h, past, lam_init, w_in, b_fox_f, g_fox_out, w_gdn_conv, gdn_a_log, gdn_dt_bias,
              g_gdn_out, g_ret_out, lq1, lk1, lq2, lk2, g_diff_out, w_out):
    B, L, _ = h.shape
    dt = h.dtype
    (z_fq, z_fk, z_fv, z_ff, z_gq, z_gk, z_gv, z_ga, z_gb, z_gz,
     z_rq, z_rk, z_rv, z_rz, z_dq, z_dk, z_dv) = split_columns(jnp.einsum('bld,de->ble', h, w_in))

    q_a, k_a, v_a = heads(z_fq, H_FOX), heads(z_fk, H_FOX), heads(z_fv, H_FOX)
    logf = jax.nn.log_sigmoid(z_ff.astype(F32) + b_fox_f.astype(F32))

    conv_prev = jnp.zeros((B, GDN_CONV - 1, 3 * W_GDN), dt) if past is None else past[6]
    qkv, conv_new = short_conv(jnp.concatenate([z_gq, z_gk, z_gv], axis=-1), conv_prev, w_gdn_conv)
    q_b, k_b, v_b = jnp.split(qkv.astype(F32), 3, axis=-1)
    q_b = l2norm(heads(q_b, H_GDN)) * HEAD_DIM ** -0.5
    k_b = l2norm(heads(k_b, H_GDN))
    v_b = heads(v_b, H_GDN)
    g_b = -jnp.exp(gdn_a_log.astype(F32)) * jax.nn.softplus(z_ga.astype(F32) + gdn_dt_bias.astype(F32))
    beta_b = jax.nn.sigmoid(z_gb.astype(F32))
    S_b0 = jnp.zeros((B, H_GDN, HEAD_DIM, HEAD_DIM), F32) if past is None else past[5].astype(F32)
    o_b, S_b = gated_delta_rule(q_b, k_b, v_b, g_b, beta_b, S_b0)

    q_c = heads(z_rq, H_RET).astype(F32)
    k_c = heads(z_rk, H_RET).astype(F32) * RET_QK ** -0.5
    v_c = heads(z_rv, H_RET).astype(F32)
    S_c0 = jnp.zeros((B, H_RET, RET_QK, HEAD_DIM), F32) if past is None else past[7].astype(F32)
    o_c, S_c = retention(q_c, k_c, v_c, S_c0, retention_log_decay())

    lam = (jnp.exp(jnp.sum(lq1.astype(F32) * lk1.astype(F32)))
           - jnp.exp(jnp.sum(lq2.astype(F32) * lk2.astype(F32))) + lam_init)
    q_d, k_d, v_d = heads(z_dq, H_DIFF), heads(z_dk, H_DIFF), heads(z_dv, H_DIFF)
    slopes = alibi_slopes(H_DIFF)

    if past is None:
        o_a = fox_prompt(q_a, k_a, v_a, logf)
        o_d = diff_prompt(q_d, k_d, v_d, lam, slopes)
    else:
        o_a = fox_sample(q_a, k_a, v_a, logf, past[0], past[1], past[2])
        o_d = diff_sample(q_d, k_d, v_d, lam, slopes, past[3], past[4])

    out_a = rmsnorm(o_a, g_fox_out)
    out_b = rmsnorm(o_b, g_gdn_out) * jax.nn.silu(heads(z_gz, H_GDN).astype(F32))
    out_c = rmsnorm(o_c, g_ret_out) * jax.nn.silu(heads(z_rz, H_RET).astype(F32))
    out_d = rmsnorm(o_d, g_diff_out) * (1.0 - lam_init)
    o = jnp.concatenate([t.reshape(B, L, -1).astype(dt) for t in (out_a, out_b, out_c, out_d)], axis=-1)
    y = jnp.einsum('blm,md->bld', o, w_out)
    entries = (k_a, v_a, logf.astype(dt), k_d, v_d, S_b.astype(dt), conv_new, S_c.astype(dt))
    return y, entries


def hier_moe(h, w_rg, b_rg, w_re, b_re, w_gate, w_up, w_down):
    lg = jnp.einsum('bld,dg->blg', h, w_rg, preferred_element_type=F32) + b_rg.astype(F32)
    p_group = jax.nn.softmax(lg, axis=-1)
    grp = jnp.argmax(lg, axis=-1)
    p_sel = jnp.max(p_group, axis=-1, keepdims=True)
    le = jnp.einsum('bld,dge->blge', h, w_re, preferred_element_type=F32) + b_re.astype(F32)
    le_g = jnp.einsum('blge,blg->ble', le, jax.nn.one_hot(grp, N_GROUPS, dtype=F32))
    top_v, top_i = lax.top_k(le_g, TOP_K)
    w_top = jax.nn.softmax(top_v, axis=-1) * p_sel
    eid = grp[..., None] * E_PER_GROUP + top_i
    gates = jnp.einsum('blk,blke->ble', w_top, jax.nn.one_hot(eid, N_EXPERTS, dtype=F32))
    a = jnp.einsum('bld,edf->blef', h, w_gate)
    u = jnp.einsum('bld,edf->blef', h, w_up)
    act = jax.nn.silu(a) * u * gates[..., None].astype(h.dtype)
    return jnp.einsum('blef,efd->bld', act, w_down)


def stack_layer_entries(entries, i):
    return jnp.stack([e[i] for e in entries])


def setup_inputs(seed: int = 0) -> dict:
    key = jax.random.key(seed)
    ks = jax.random.split(key, 40)
    n_pages = PAST_LEN // PAGE_SIZE
    n_pool = (DEC_BATCH * n_pages * POOL_NUM) // POOL_DEN

    def nrm(k, shape, scale=1.0):
        return scale * jax.random.normal(k, shape, F32)

    def gain(k, shape):
        return 1.0 + 0.02 * jax.random.normal(k, shape, F32)

    page_table = jax.random.permutation(ks[10], n_pool)[:DEC_BATCH * n_pages]
    page_table = page_table.reshape(DEC_BATCH, n_pages).astype(jnp.int32)
    dt = jnp.exp(jax.random.uniform(ks[16], (DEPTH, H_GDN), F32, math.log(1e-3), math.log(1e-1)))
    return {
        'x_prompt': nrm(ks[0], (BATCH, SEQ, D_MODEL)),
        'x_sample': nrm(ks[1], (DEC_BATCH, DEC_SEQ, D_MODEL)),
        'cache_fox_k': nrm(ks[2], (DEPTH, n_pool, PAGE_SIZE, H_FOX, HEAD_DIM)),
        'cache_fox_v': nrm(ks[3], (DEPTH, n_pool, PAGE_SIZE, H_FOX, HEAD_DIM)),
        'cache_fox_logf': jax.nn.log_sigmoid(FOX_BIAS_INIT + nrm(ks[4], (DEPTH, n_pool, PAGE_SIZE, H_FOX), 0.5)),
        'cache_diff_k': nrm(ks[5], (DEPTH, n_pool, PAGE_SIZE, H_DIFF, 2 * DIFF_QK)),
        'cache_diff_v': nrm(ks[6], (DEPTH, n_pool, PAGE_SIZE, H_DIFF, HEAD_DIM)),
        'state_gdn': nrm(ks[7], (DEPTH, DEC_BATCH, H_GDN, HEAD_DIM, HEAD_DIM), 0.1),
        'state_gdn_conv': nrm(ks[8], (DEPTH, DEC_BATCH, GDN_CONV - 1, 3 * W_GDN)),
        'state_ret': nrm(ks[9], (DEPTH, DEC_BATCH, H_RET, RET_QK, HEAD_DIM), 0.5),
        'page_table': page_table,
        'g_mix_norm': gain(ks[11], (DEPTH, D_MODEL)),
        'w_in': nrm(ks[12], (DEPTH, D_MODEL, D_IN), D_MODEL ** -0.5),
        'b_fox_f': FOX_BIAS_INIT + nrm(ks[13], (DEPTH, H_FOX), 0.5),
        'g_fox_out': gain(ks[14], (DEPTH, H_FOX, HEAD_DIM)),
        'w_gdn_conv': nrm(ks[15], (DEPTH, GDN_CONV, 3 * W_GDN), GDN_CONV ** -0.5),
        'gdn_a_log': jnp.log(jax.random.uniform(ks[17], (DEPTH, H_GDN), F32, 1.0, 16.0)),
        'gdn_dt_bias': dt + jnp.log(-jnp.expm1(-dt)),
        'g_gdn_out': gain(ks[18], (DEPTH, HEAD_DIM)),
        'g_ret_out': gain(ks[19], (DEPTH, H_RET, HEAD_DIM)),
        'diff_lambda_q1': nrm(ks[20], (DEPTH, DIFF_QK), 0.1),
        'diff_lambda_k1': nrm(ks[21], (DEPTH, DIFF_QK), 0.1),
        'diff_lambda_q2': nrm(ks[22], (DEPTH, DIFF_QK), 0.1),
        'diff_lambda_k2': nrm(ks[23], (DEPTH, DIFF_QK), 0.1),
        'g_diff_out': gain(ks[24], (DEPTH, HEAD_DIM)),
        'w_out': nrm(ks[25], (DEPTH, D_MIX, D_MODEL), D_MIX ** -0.5),
        'g_ffn_norm': gain(ks[26], (DEPTH, D_MODEL)),
        'w_router_group': nrm(ks[27], (DEPTH, D_MODEL, N_GROUPS), D_MODEL ** -0.5),
        'b_router_group': nrm(ks[28], (DEPTH, N_GROUPS), 0.01),
        'w_router_expert': nrm(ks[29], (DEPTH, D_MODEL, N_GROUPS, E_PER_GROUP), D_MODEL ** -0.5),
        'b_router_expert': nrm(ks[30], (DEPTH, N_GROUPS, E_PER_GROUP), 0.01),
        'w_exp_gate': nrm(ks[31], (DEPTH, N_EXPERTS, D_MODEL, D_FF_EXPERT), D_MODEL ** -0.5),
        'w_exp_up': nrm(ks[32], (DEPTH, N_EXPERTS, D_MODEL, D_FF_EXPERT), D_MODEL ** -0.5),
        'w_exp_down': nrm(ks[33], (DEPTH, N_EXPERTS, D_FF_EXPERT, D_MODEL), D_FF_EXPERT ** -0.5),
        'g_final': gain(ks[34], (D_MODEL,)),
    }


def reference(x_prompt, x_sample, cache_fox_k, cache_fox_v, cache_fox_logf, cache_diff_k, cache_diff_v,
              state_gdn, state_gdn_conv, state_ret, page_table,
              g_mix_norm, w_in, b_fox_f, g_fox_out, w_gdn_conv, gdn_a_log, gdn_dt_bias, g_gdn_out,
              g_ret_out, diff_lambda_q1, diff_lambda_k1, diff_lambda_q2, diff_lambda_k2, g_diff_out,
              w_out, g_ffn_norm, w_router_group, b_router_group, w_router_expert, b_router_expert,
              w_exp_gate, w_exp_up, w_exp_down, g_final):
    yp, ys = x_prompt, x_sample
    ent_p, ent_s = [], []
    for l in range(DEPTH):
        lam_init = 0.8 - 0.6 * math.exp(-0.3 * l)
        wl = (w_in[l], b_fox_f[l], g_fox_out[l], w_gdn_conv[l], gdn_a_log[l], gdn_dt_bias[l],
              g_gdn_out[l], g_ret_out[l], diff_lambda_q1[l], diff_lambda_k1[l], diff_lambda_q2[l],
              diff_lambda_k2[l], g_diff_out[l], w_out[l])
        past = (gather_pages(cache_fox_k[l], page_table), gather_pages(cache_fox_v[l], page_table),
                gather_pages(cache_fox_logf[l], page_table), gather_pages(cache_diff_k[l], page_table),
                gather_pages(cache_diff_v[l], page_table), state_gdn[l], state_gdn_conv[l], state_ret[l])
        mp, ep = token_mix(rmsnorm(yp, g_mix_norm[l]), None, lam_init, *wl)
        ms, es = token_mix(rmsnorm(ys, g_mix_norm[l]), past, lam_init, *wl)
        yp = yp + mp
        ys = ys + ms
        fl = (w_router_group[l], b_router_group[l], w_router_expert[l], b_router_expert[l],
              w_exp_gate[l], w_exp_up[l], w_exp_down[l])
        yp = yp + hier_moe(rmsnorm(yp, g_ffn_norm[l]), *fl)
        ys = ys + hier_moe(rmsnorm(ys, g_ffn_norm[l]), *fl)
        ent_p.append(ep)
        ent_s.append(es)
    y_prompt = rmsnorm(yp, g_final)
    y_sample = rmsnorm(ys, g_final)
    p_fox_k = stack_layer_entries(ent_p, 0)
    p_fox_v = stack_layer_entries(ent_p, 1)
    p_fox_logf = stack_layer_entries(ent_p, 2)
    p_diff_k = stack_layer_entries(ent_p, 3)
    p_diff_v = stack_layer_entries(ent_p, 4)
    p_gdn_state = stack_layer_entries(ent_p, 5)
    p_gdn_conv = stack_layer_entries(ent_p, 6)
    p_ret_state = stack_layer_entries(ent_p, 7)
    s_fox_k = stack_layer_entries(ent_s, 0)
    s_fox_v = stack_layer_entries(ent_s, 1)
    s_fox_logf = stack_layer_entries(ent_s, 2)
    s_diff_k = stack_layer_entries(ent_s, 3)
    s_diff_v = stack_layer_entries(ent_s, 4)
    s_gdn_state = stack_layer_entries(ent_s, 5)
    s_gdn_conv = stack_layer_entries(ent_s, 6)
    s_ret_state = stack_layer_entries(ent_s, 7)
    return (y_prompt, y_sample, p_fox_k, p_fox_v, p_fox_logf, p_diff_k, p_diff_v, p_gdn_state, p_gdn_conv,
            p_ret_state, s_fox_k, s_fox_v, s_fox_logf, s_diff_k, s_diff_v, s_gdn_state, s_gdn_conv, s_ret_state)
```

```python
import functools
import math

import numpy as np
import jax
import jax.numpy as jnp
from jax import lax
from jax.experimental import pallas as pl
from jax.experimental.pallas import tpu as pltpu

F32 = jnp.float32
BF16 = jnp.bfloat16
HI = lax.Precision.HIGHEST

D_MODEL = 1024
HEAD_DIM = 64
N_HEADS = 4
W_MIX = N_HEADS * HEAD_DIM
RET_QK = 32
DIFF_QK = 32
GDN_CONV = 4
GDN_CHUNK = 64
RET_CHUNK = 128
PAGE = 128
N_GROUPS = 4
E_PER_GROUP = 4
N_EXPERTS = 16
D_FF = 256
RMS_EPS = 1e-6
LOG2E = 1.4426950408889634
NEG = -1e30
LANES = 128
VMEM_LIMIT = 48 * 1024 * 1024

OFF_FQ, OFF_FK, OFF_FV = 0, 256, 512
OFF_GQKV, OFF_GZ = 768, 1536
OFF_RQ, OFF_RK, OFF_RV, OFF_RZ = 1792, 1920, 2048, 2304
OFF_DQ, OFF_DK, OFF_DV = 2560, 2816, 3072
OFF_SMALL = 3328
W_IN_COLS = 3456

SC_FOX = HEAD_DIM ** -0.5 * LOG2E
SC_DIFF = DIFF_QK ** -0.5 * LOG2E

_PIECES = (
    (OFF_FQ, 256, ((SC_FOX, BF16),)),
    (OFF_FK, 256, ((1.0, F32), (1.0, BF16))),
    (OFF_FV, 256, ((1.0, F32), (1.0, BF16))),
    (OFF_GQKV, 768, ((1.0, F32),)),
    (OFF_GZ, 256, ((1.0, F32),)),
    (OFF_RQ, 128, ((1.0, F32),)),
    (OFF_RK, 128, ((RET_QK ** -0.5, F32),)),
    (OFF_RV, 256, ((1.0, F32),)),
    (OFF_RZ, 256, ((1.0, F32),)),
    (OFF_DQ, 256, ((SC_DIFF, BF16),)),
    (OFF_DK, 256, ((1.0, F32), (1.0, BF16))),
    (OFF_DV, 256, ((1.0, F32), (1.0, BF16))),
)
(I_FQ, I_FK, I_FKB, I_FV, I_FVB, I_GQKV, I_GZ, I_RQ, I_RK, I_RV, I_RZ,
 I_DQ, I_DK, I_DKB, I_DV, I_DVB, I_SMALL, I_SMALLT, I_CT) = range(19)


def _nt(a, b, precision=None):
    return lax.dot_general(a, b, (((1,), (1,)), ((), ())), precision=precision,
                           preferred_element_type=F32)


def _mm(a, b, precision=None):
    return jnp.dot(a, b, precision=precision, preferred_element_type=F32)


def _gate_values(z, kind, a_log):
    t = jnp.log1p(jnp.exp(-jnp.abs(z)))
    logf = jnp.minimum(z, 0.0) - t
    g = -jnp.exp(a_log) * (jnp.maximum(z, 0.0) + t)
    beta = 1.0 / (1.0 + jnp.exp(-z))
    return jnp.where(kind == 0, logf, jnp.where(kind == 1, g, jnp.where(kind == 2, beta, 0.0)))


def _inproj_kernel(x_ref, g_ref, w_ref, wst_ref, brow_ref, arow_ref, bcol_ref, acol_ref, tri_ref,
                   *refs):
    outs = refs[:19]
    carry_ref = refs[19]
    i = pl.program_id(0)
    tm = x_ref.shape[0]

    @pl.when(i == 0)
    def _():
        carry_ref[...] = jnp.zeros_like(carry_ref)

    x = x_ref[...]
    ms = jnp.mean(x * x, axis=-1, keepdims=True)
    h = (x * lax.rsqrt(ms + RMS_EPS) * g_ref[...]).astype(BF16)

    k = 0
    for off, wd, outs_spec in _PIECES:
        z = _mm(h, w_ref[:, off:off + wd])
        for scale, _ in outs_spec:
            zz = z if scale == 1.0 else z * scale
            outs[k][...] = zz.astype(outs[k].dtype)
            k += 1

    zs = _mm(h, w_ref[:, OFF_SMALL:OFF_SMALL + LANES]) + brow_ref[...]
    kind = lax.broadcasted_iota(jnp.int32, zs.shape, 1) // 4
    outs[I_SMALL][...] = _gate_values(zs, kind, arow_ref[...])

    zt = _nt(wst_ref[...], h) + bcol_ref[:, 0:1]
    kind_t = lax.broadcasted_iota(jnp.int32, zt.shape, 0) // 4
    vt = _gate_values(zt, kind_t, acol_ref[:, 0:1])
    outs[I_SMALLT][...] = vt
    ct = _mm(vt, tri_ref[...], HI) + carry_ref[:, 0:1]
    outs[I_CT][...] = ct
    carry_ref[...] = jnp.broadcast_to(ct[:, tm - 1:tm], carry_ref.shape)


def _inproj(x2d, gain, w_arr, wst, brow, arow, bcol, acol, tm):
    rows = x2d.shape[0]
    tri = jnp.triu(jnp.ones((tm, tm), F32))
    shapes, specs = [], []
    for _, wd, outs_spec in _PIECES:
        for _, dt in outs_spec:
            shapes.append(jax.ShapeDtypeStruct((rows, wd), dt))
            specs.append(pl.BlockSpec((tm, wd), lambda i: (i, 0)))
    shapes.append(jax.ShapeDtypeStruct((rows, LANES), F32))
    specs.append(pl.BlockSpec((tm, LANES), lambda i: (i, 0)))
    for _ in range(2):
        shapes.append(jax.ShapeDtypeStruct((16, rows), F32))
        specs.append(pl.BlockSpec((16, tm), lambda i: (0, i)))
    const = lambda i: (0, 0)
    return pl.pallas_call(
        _inproj_kernel,
        grid=(rows // tm,),
        in_specs=[
            pl.BlockSpec((tm, D_MODEL), lambda i: (i, 0)),
            pl.BlockSpec((1, D_MODEL), const),
            pl.BlockSpec((D_MODEL, W_IN_COLS), const),
            pl.BlockSpec((16, D_MODEL), const),
            pl.BlockSpec((1, LANES), const),
            pl.BlockSpec((1, LANES), const),
            pl.BlockSpec((16, LANES), const),
            pl.BlockSpec((16, LANES), const),
            pl.BlockSpec((tm, tm), const),
        ],
        out_specs=specs,
        out_shape=shapes,
        scratch_shapes=[pltpu.VMEM((16, LANES), F32)],
        compiler_params=pltpu.CompilerParams(dimension_semantics=("arbitrary",),
                                             vmem_limit_bytes=VMEM_LIMIT),
        name="inproj",
    )(x2d, gain, w_arr, wst, brow, arow, bcol, acol, tri)


def _softmax_step(s, m, acc, v):
    m_new = jnp.maximum(m, jnp.max(s, axis=-1, keepdims=True))
    alpha = jnp.exp2(m - m_new)
    p = jnp.exp2(s - m_new)
    acc = alpha * acc + _mm(p.astype(BF16), v)
    return m_new, acc


def _finish_pair(o, sel, half, o_ref):
    @pl.when(half == 0)
    def _():
        o_ref[...] = jnp.where(sel, o, 0.0)

    @pl.when(half == 1)
    def _():
        o_ref[...] = jnp.where(sel, o, o_ref[...])


def _fox_kernel(q_ref, k_ref, v_ref, c_ref, o_ref, *, T):
    qi = pl.program_id(1)
    half = pl.program_id(2)
    lane = lax.broadcasted_iota(jnp.int32, (1, LANES), 1)
    sel = (lane // HEAD_DIM) == half
    q = q_ref[...]
    q = jnp.where(sel, q, jnp.zeros_like(q))
    c0 = c_ref[qi][:, 0:1]
    row = lax.broadcasted_iota(jnp.int32, (T, T), 0)
    col = lax.broadcasted_iota(jnp.int32, (T, T), 1)

    def step(j, carry, diag):
        off = pl.multiple_of(j * T, T)
        kj = k_ref[pl.ds(off, T), :]
        vj = v_ref[pl.ds(off, T), :]
        vj = jnp.where(sel, vj, jnp.ones_like(vj))
        s = _nt(q, kj) + (c0 - c_ref[j]) * LOG2E
        if diag:
            s = jnp.where(col <= row, s, NEG)
        return _softmax_step(s, *carry, vj)

    init = (jnp.full((T, 1), NEG, F32), jnp.zeros((T, LANES), F32))
    carry = lax.fori_loop(0, qi, lambda j, c: step(j, c, False), init)
    _, acc = step(qi, carry, True)
    l = jnp.max(jnp.where(sel, 0.0, acc), axis=-1, keepdims=True)
    _finish_pair(acc * (1.0 / l), sel, half, o_ref)


def _fox_prompt(qb, kb, vb, c4, T):
    S = qb.shape[0]
    nk = S // T
    cc = c4.reshape(N_HEADS, nk, 1, T)
    return pl.pallas_call(
        functools.partial(_fox_kernel, T=T),
        grid=(2, nk, 2),
        in_specs=[
            pl.BlockSpec((T, LANES), lambda p, i, h: (i, p)),
            pl.BlockSpec((S, LANES), lambda p, i, h: (0, p)),
            pl.BlockSpec((S, LANES), lambda p, i, h: (0, p)),
            pl.BlockSpec((None, nk, 1, T), lambda p, i, h: (2 * p + h, 0, 0, 0)),
        ],
        out_specs=pl.BlockSpec((T, LANES), lambda p, i, h: (i, p)),
        out_shape=jax.ShapeDtypeStruct((S, W_MIX), F32),
        compiler_params=pltpu.CompilerParams(
            dimension_semantics=("arbitrary", "arbitrary", "arbitrary"),
            vmem_limit_bytes=VMEM_LIMIT),
        name="fox_prompt",
    )(qb, kb, vb, cc)


def _diff_lambda(lp_ref, lam_init):
    lp = lp_ref[...]
    a = jnp.sum(lp[0:1] * lp[1:2], axis=-1, keepdims=True)
    b = jnp.sum(lp[2:3] * lp[3:4], axis=-1, keepdims=True)
    return jnp.exp(a) - jnp.exp(b) + lam_init


def _diff_kernel(slopes_ref, q_ref, k_ref, v_ref, lp_ref, o_ref, *, T, lam_init):
    pair = pl.program_id(0)
    qi = pl.program_id(1)
    half = pl.program_id(2)
    lane = lax.broadcasted_iota(jnp.int32, (1, LANES), 1)
    sel = (lane // HEAD_DIM) == half
    sel1 = (lane // DIFF_QK) == 2 * half
    sel2 = (lane // DIFF_QK) == 2 * half + 1
    q = q_ref[...]
    zero = jnp.zeros_like(q)
    q12 = jnp.concatenate([jnp.where(sel1, q, zero), jnp.where(sel2, q, zero)], axis=0)
    slope = slopes_ref[2 * pair + half] * LOG2E
    row = lax.broadcasted_iota(jnp.int32, (2 * T, T), 0) % T
    col = lax.broadcasted_iota(jnp.int32, (2 * T, T), 1)
    kcol = lax.broadcasted_iota(jnp.int32, (1, T), 1)

    def step(j, carry, diag):
        off = pl.multiple_of(j * T, T)
        kj = k_ref[pl.ds(off, T), :]
        vj = v_ref[pl.ds(off, T), :]
        vj = jnp.where(sel, vj, jnp.ones_like(vj))
        s = _nt(q12, kj) + slope * ((j - qi) * T + kcol).astype(F32)
        if diag:
            s = jnp.where(col <= row, s, NEG)
        return _softmax_step(s, *carry, vj)

    init = (jnp.full((2 * T, 1), NEG, F32), jnp.zeros((2 * T, LANES), F32))
    carry = lax.fori_loop(0, qi, lambda j, c: step(j, c, False), init)
    _, acc = step(qi, carry, True)
    l = jnp.max(jnp.where(sel, 0.0, acc), axis=-1, keepdims=True)
    o = acc * (1.0 / l)
    lam = _diff_lambda(lp_ref, lam_init)
    _finish_pair(o[:T] - lam * o[T:], sel, half, o_ref)


def _diff_prompt(qb, kb, vb, slopes, lam_params, lam_init, T):
    S = qb.shape[0]
    nk = S // T
    return pl.pallas_call(
        functools.partial(_diff_kernel, T=T, lam_init=lam_init),
        grid=(2, nk, 2),
        in_specs=[
            pl.BlockSpec(memory_space=pltpu.SMEM),
            pl.BlockSpec((T, LANES), lambda p, i, h: (i, p)),
            pl.BlockSpec((S, LANES), lambda p, i, h: (0, p)),
            pl.BlockSpec((S, LANES), lambda p, i, h: (0, p)),
            pl.BlockSpec((8, LANES), lambda p, i, h: (0, 0)),
        ],
        out_specs=pl.BlockSpec((T, LANES), lambda p, i, h: (i, p)),
        out_shape=jax.ShapeDtypeStruct((S, W_MIX), F32),
        compiler_params=pltpu.CompilerParams(
            dimension_semantics=("arbitrary", "arbitrary", "arbitrary"),
            vmem_limit_bytes=VMEM_LIMIT),
        name="diff_prompt",
    )(slopes, qb, kb, vb, lam_params)


def _pick_heads(acc32):
    lane = lax.broadcasted_iota(jnp.int32, (8, W_MIX), 1)
    out = jnp.zeros((8, W_MIX), F32)
    for h in range(N_HEADS):
        out = jnp.where((lane // HEAD_DIM) == h, acc32[8 * h:8 * h + 8], out)
    return out


def _split3(x):
    hi = x.astype(BF16).astype(F32)
    r1 = x - hi
    mid = r1.astype(BF16).astype(F32)
    return hi, mid, r1 - mid


def _decode_kernel(pt_ref, fq_ref, dq_ref, *refs, P, n_steps, n_pages, lam_init):
    fk, fv, fl, dk, dv = (refs[i * P:(i + 1) * P] for i in range(5))
    fkn_ref, fvn_ref, flogn_ref, dkn_ref, dvn_ref, tri_ref, lp_ref, of_ref, od_ref = refs[5 * P:5 * P + 9]
    qf_s, qd_s, mf_s, lf_s, af_s, md_s, ld_s, ad_s, cc_s = refs[5 * P + 9:]
    j = pl.program_id(1)
    lane8 = lax.broadcasted_iota(jnp.int32, (8, W_MIX), 1)

    @pl.when(j == 0)
    def _():
        fq = fq_ref[...]
        dq = dq_ref[...]
        qf_s[...] = jnp.concatenate(
            [jnp.where((lane8 // HEAD_DIM) == h, fq, 0.0) for h in range(N_HEADS)], axis=0).astype(BF16)
        d1 = [jnp.where((lane8 // DIFF_QK) == 2 * h, dq, 0.0) for h in range(N_HEADS)]
        d2 = [jnp.where((lane8 // DIFF_QK) == 2 * h + 1, dq, 0.0) for h in range(N_HEADS)]
        qd_s[...] = jnp.concatenate(d1 + d2, axis=0).astype(BF16)
        mf_s[...] = jnp.full(mf_s.shape, NEG, F32)
        lf_s[...] = jnp.zeros_like(lf_s)
        af_s[...] = jnp.zeros_like(af_s)
        md_s[...] = jnp.full(md_s.shape, NEG, F32)
        ld_s[...] = jnp.zeros_like(ld_s)
        ad_s[...] = jnp.zeros_like(ad_s)
        cc_s[...] = jnp.zeros_like(cc_s)

    tri = tri_ref[...]
    r64 = lax.broadcasted_iota(jnp.int32, (64, 1), 0)
    hd = (r64 % 32) // 8
    slope64 = jnp.exp2(-2.0 * (hd + 1).astype(F32)) * LOG2E
    kcol = lax.broadcasted_iota(jnp.int32, (1, PAGE), 1)
    zero8 = jnp.zeros((8, PAGE), F32)

    def expand_heads(c8):
        return jnp.concatenate([jnp.broadcast_to(c8[h:h + 1], (8, PAGE)) for h in range(N_HEADS)], axis=0)

    def cumsum_lanes(xs):
        parts = []
        for x in xs:
            parts += list(_split3(x)) + [zero8]
        loc = _mm(jnp.concatenate(parts, axis=0).astype(BF16), tri)
        return [loc[32 * i:32 * i + 8] + loc[32 * i + 8:32 * i + 16] + loc[32 * i + 16:32 * i + 24]
                for i in range(len(xs))]

    def update(s, vts, m_s, l_s, a_s):
        m_old = m_s[...]
        m_new = jnp.maximum(m_old, jnp.max(s, axis=-1, keepdims=True))
        alpha = jnp.exp2(m_old - m_new)
        p = jnp.exp2(s - m_new)
        l_s[...] = alpha * l_s[...] + jnp.sum(p, axis=-1, keepdims=True)
        acc = alpha * a_s[...]
        for i, vt in enumerate(vts):
            acc = acc + _nt(p[:, i * PAGE:(i + 1) * PAGE].astype(BF16), vt)
        m_s[...] = m_new
        a_s[...] = acc

    locs = cumsum_lanes([fl[i][...] for i in range(P)])
    carry = cc_s[:, 0:1]
    sf, sd = [], []
    for i in range(P):
        cum = locs[i] + carry
        carry = cum[:, PAGE - 1:PAGE]
        sf.append(_mm(qf_s[...], fk[i][...].astype(BF16)) - expand_heads(cum) * LOG2E)
        pos = ((j * P + i) - n_pages) * PAGE + kcol
        sd.append(_mm(qd_s[...], dk[i][...].astype(BF16)) + slope64 * pos.astype(F32))
    cc_s[...] = jnp.broadcast_to(carry, cc_s.shape)
    update(jnp.concatenate(sf, axis=1), [fv[i][...].astype(BF16) for i in range(P)], mf_s, lf_s, af_s)
    update(jnp.concatenate(sd, axis=1), [dv[i][...].astype(BF16) for i in range(P)], md_s, ld_s, ad_s)

    @pl.when(j == n_steps - 1)
    def _():
        cum_n = cumsum_lanes([flogn_ref[...]])[0] + cc_s[:, 0:1]
        r32 = lax.broadcasted_iota(jnp.int32, (32, PAGE), 0)
        c32 = lax.broadcasted_iota(jnp.int32, (32, PAGE), 1)
        sfn = _mm(qf_s[...], fkn_ref[...]) - expand_heads(cum_n) * LOG2E
        update(jnp.where(c32 <= (r32 % 8), sfn, NEG), [fvn_ref[...]], mf_s, lf_s, af_s)
        r = lax.broadcasted_iota(jnp.int32, (64, PAGE), 0)
        c = lax.broadcasted_iota(jnp.int32, (64, PAGE), 1)
        sdn = _mm(qd_s[...], dkn_ref[...]) + slope64 * kcol.astype(F32)
        update(jnp.where(c <= (r % 8), sdn, NEG), [dvn_ref[...]], md_s, ld_s, ad_s)

        of_ref[...] = _pick_heads(af_s[...] * (1.0 / lf_s[...]))
        od = ad_s[...] * (1.0 / ld_s[...])
        lam = _diff_lambda(lp_ref, lam_init)
        od_ref[...] = _pick_heads(od[0:32] - lam * od[32:64])


def _decode(page_table, layer, fq8, dq8, fk_t, fv_t, flog_t, dk_t, dv_t,
            fkn, fvn, flogn, dkn, dvn, lam_params, lam_init):
    n, n_pages = page_table.shape
    P = 8
    while n_pages % P:
        P //= 2
    n_steps = n_pages // P
    tri = jnp.triu(jnp.ones((PAGE, PAGE), BF16))
    seq = lambda b, j, pt: (b, 0, 0)
    const = lambda b, j, pt: (0, 0)

    def cache_spec(i):
        return pl.BlockSpec((None, None, W_MIX, PAGE), lambda b, j, pt: (layer, pt[b, j * P + i], 0, 0))

    def flog_spec(i):
        return pl.BlockSpec((None, 8, PAGE), lambda b, j, pt: (pt[b, j * P + i], 0, 0))

    pages = lambda: [cache_spec(i) for i in range(P)]
    grid_spec = pltpu.PrefetchScalarGridSpec(
        num_scalar_prefetch=1,
        grid=(n, n_steps),
        in_specs=[pl.BlockSpec((None, 8, W_MIX), seq), pl.BlockSpec((None, 8, W_MIX), seq)]
        + pages() + pages() + [flog_spec(i) for i in range(P)] + pages() + pages() + [
            pl.BlockSpec((None, W_MIX, PAGE), seq),
            pl.BlockSpec((None, W_MIX, PAGE), seq),
            pl.BlockSpec((None, 8, PAGE), seq),
            pl.BlockSpec((None, W_MIX, PAGE), seq),
            pl.BlockSpec((None, W_MIX, PAGE), seq),
            pl.BlockSpec((PAGE, PAGE), const),
            pl.BlockSpec((8, LANES), const),
        ],
        out_specs=[pl.BlockSpec((None, 8, W_MIX), seq), pl.BlockSpec((None, 8, W_MIX), seq)],
        scratch_shapes=[
            pltpu.VMEM((32, W_MIX), BF16), pltpu.VMEM((64, W_MIX), BF16),
            pltpu.VMEM((32, 1), F32), pltpu.VMEM((32, 1), F32), pltpu.VMEM((32, W_MIX), F32),
            pltpu.VMEM((64, 1), F32), pltpu.VMEM((64, 1), F32), pltpu.VMEM((64, W_MIX), F32),
            pltpu.VMEM((8, LANES), F32),
        ],
    )
    return pl.pallas_call(
        functools.partial(_decode_kernel, P=P, n_steps=n_steps, n_pages=n_pages, lam_init=lam_init),
        grid_spec=grid_spec,
        out_shape=[jax.ShapeDtypeStruct((n, 8, W_MIX), F32), jax.ShapeDtypeStruct((n, 8, W_MIX), F32)],
        compiler_params=pltpu.CompilerParams(dimension_semantics=("arbitrary", "arbitrary"),
                                             vmem_limit_bytes=VMEM_LIMIT),
        name="paged_decode",
    )(page_table, fq8, dq8, *([fk_t] * P), *([fv_t] * P), *([flog_t] * P), *([dk_t] * P), *([dv_t] * P),
      fkn, fvn, flogn, dkn, dvn, tri, lam_params)


def _mm_b(a, b):
    return _mm(a.astype(BF16), b.astype(BF16))


def _nt_b(a, b):
    return _nt(a.astype(BF16), b.astype(BF16))


def _mm_x2(a, b):
    ah = a.astype(BF16)
    al = (a - ah.astype(F32)).astype(BF16)
    bh = b.astype(BF16)
    bl = (b - bh.astype(F32)).astype(BF16)
    return _mm(ah, bh) + (_mm(ah, bl) + _mm(al, bh))


def _mm01_rows(x, b01):
    m = x.shape[0]
    r = _mm(jnp.concatenate(_split3(x), axis=0).astype(BF16), b01)
    return r[0:m] + r[m:2 * m] + r[2 * m:3 * m]


def _mm01_cols(a01, x):
    n = x.shape[1]
    r = _mm(a01, jnp.concatenate(_split3(x), axis=1).astype(BF16))
    return r[:, 0:n] + r[:, n:2 * n] + r[:, 2 * n:3 * n]


def _inv_unit_lower(a, C):
    row = lax.broadcasted_iota(jnp.int32, (C, C), 0)
    col = lax.broadcasted_iota(jnp.int32, (C, C), 1)
    eye = (row == col).astype(F32)
    n1 = jnp.where((row // 8) == (col // 8), a, 0.0)
    n2 = _mm_x2(n1, n1)
    n4 = _mm_x2(n2, n2)
    d = eye - n1
    d = d + _mm_x2(d, n2)
    d = d + _mm_x2(d, n4)
    b = 8
    while b < C:
        off = jnp.where(((row // (2 * b)) == (col // (2 * b))) & ((row // b) != (col // b)), a, 0.0)
        d = d - _mm_x2(d, _mm_x2(off, d))
        b *= 2
    return d


def _gdn_kernel(qkv_ref, prev_ref, small_ref, smallt_ref, convw_ref, s0_ref, bd_ref, eg_ref, eb_ref,
                o_ref, sout_ref, xbuf, sbd, *, C, CPS, nvalid):
    step = pl.program_id(1)
    R = C * CPS

    @pl.when(step == 0)
    def _():
        xbuf[0:8, :] = prev_ref[...]
        sbd[...] = s0_ref[...]

    xbuf[8:8 + R, :] = qkv_ref[...]
    y = xbuf[5:5 + R, :] * convw_ref[0:1, :]
    for t in range(1, GDN_CONV):
        y = y + xbuf[5 + t:5 + t + R, :] * convw_ref[t:t + 1, :]
    y = y * (1.0 / (1.0 + jnp.exp(-y)))
    tail = xbuf[R:R + 8, :]
    xbuf[0:8, :] = tail

    bd = bd_ref[...]
    bdmask = bd > 0.5
    bd16 = bd.astype(BF16)
    egb16 = jnp.concatenate([eg_ref[...], eb_ref[...]], axis=1).astype(BF16)
    row = lax.broadcasted_iota(jnp.int32, (C, C), 0)
    col = lax.broadcasted_iota(jnp.int32, (C, C), 1)
    incl = row >= col
    strict = row > col
    ltri = incl.astype(F32).astype(BF16)
    utri = (row <= col).astype(F32).astype(BF16)
    lane = lax.broadcasted_iota(jnp.int32, (1, W_MIX), 1)
    rvalid = lax.broadcasted_iota(jnp.int32, (C, 1), 0) < nvalid
    cvalid = lax.broadcasted_iota(jnp.int32, (1, C), 1) < nvalid

    for c in range(CPS):
        r0 = c * C
        xq = y[r0:r0 + C, 0:W_MIX]
        xk = y[r0:r0 + C, W_MIX:2 * W_MIX]
        xv = y[r0:r0 + C, 2 * W_MIX:3 * W_MIX]
        sm = small_ref[r0:r0 + C, :]
        smt = smallt_ref[:, r0:r0 + C]
        if nvalid < C:
            xq = jnp.where(rvalid, xq, 0.0)
            xk = jnp.where(rvalid, xk, 0.0)
            xv = jnp.where(rvalid, xv, 0.0)
            sm = jnp.where(rvalid, sm, 0.0)
            smt = jnp.where(cvalid, smt, 0.0)
        qn = xq * lax.rsqrt(_mm01_rows(xq * xq, bd16) + 1e-6) * HEAD_DIM ** -0.5
        kn = xk * lax.rsqrt(_mm01_rows(xk * xk, bd16) + 1e-6)
        gb = _mm01_rows(sm, egb16)
        gexp = gb[:, 0:W_MIX]
        bexp = gb[:, W_MIX:2 * W_MIX]
        gcum = _mm01_cols(ltri, gexp)
        grow = _mm01_rows(smt, utri)
        eg = jnp.exp(gcum)
        glast = gcum[C - 1:C, :]
        kd = kn * jnp.exp(glast - gcum)
        vb = xv * bexp
        kb = kn * bexp * eg

        usel = jnp.zeros((C, W_MIX), F32)
        wsel = jnp.zeros((C, W_MIX), F32)
        qks = []
        for h in range(N_HEADS):
            mh = (lane // HEAD_DIM) == h
            rel = gcum[:, HEAD_DIM * h:HEAD_DIM * h + C] - grow[4 + h:5 + h, :]
            decay = jnp.where(incl, jnp.exp(jnp.where(incl, rel, 0.0)), 0.0)
            kk = _nt_b(jnp.where(mh, kn, 0.0), kn)
            a = jnp.where(strict, bexp[:, HEAD_DIM * h:HEAD_DIM * h + C] * kk * decay, 0.0)
            tm = _inv_unit_lower(a, C)
            usel = jnp.where(mh, _mm_b(tm, vb), usel)
            wsel = jnp.where(mh, _mm_b(tm, kb), wsel)
            qks.append(_nt_b(jnp.where(mh, qn, 0.0), kn) * decay)

        s = sbd[...]
        u = usel - _mm_b(wsel, s)
        o = _mm_b(qn * eg, s)
        for h in range(N_HEADS):
            mh = (lane // HEAD_DIM) == h
            o = o + jnp.where(mh, _mm_b(qks[h], u), 0.0)
        upd = lax.dot_general(kd.astype(BF16), u.astype(BF16), (((0,), (0,)), ((), ())),
                              preferred_element_type=F32)
        sbd[...] = s * jnp.exp(glast) + jnp.where(bdmask, upd, 0.0)
        o_ref[r0:r0 + C, :] = o

    @pl.when(step == pl.num_programs(1) - 1)
    def _():
        sout_ref[...] = sbd[...]


def _gdn(qkv, prev8, small, smallt, convw, s0, C, CPS, nvalid):
    nseq, L, _ = qkv.shape
    R = C * CPS
    nsteps = L // R
    hidx = np.arange(W_MIX) // HEAD_DIM
    bd = jnp.asarray((hidx[:, None] == hidx[None, :]).astype(np.float32))
    eg = np.zeros((LANES, W_MIX), np.float32)
    eb = np.zeros((LANES, W_MIX), np.float32)
    eg[4 + hidx, np.arange(W_MIX)] = 1.0
    eb[8 + hidx, np.arange(W_MIX)] = 1.0
    const = lambda b, i: (0, 0)
    return pl.pallas_call(
        functools.partial(_gdn_kernel, C=C, CPS=CPS, nvalid=nvalid),
        grid=(nseq, nsteps),
        in_specs=[
            pl.BlockSpec((None, R, 3 * W_MIX), lambda b, i: (b, i, 0)),
            pl.BlockSpec((None, 8, 3 * W_MIX), lambda b, i: (b, 0, 0)),
            pl.BlockSpec((None, R, LANES), lambda b, i: (b, i, 0)),
            pl.BlockSpec((None, 16, LANES), lambda b, i: (b, 0, i)),
            pl.BlockSpec((8, 3 * W_MIX), const),
            pl.BlockSpec((None, W_MIX, W_MIX), lambda b, i: (b, 0, 0)),
            pl.BlockSpec((W_MIX, W_MIX), const),
            pl.BlockSpec((LANES, W_MIX), const),
            pl.BlockSpec((LANES, W_MIX), const),
        ],
        out_specs=[
            pl.BlockSpec((None, R, W_MIX), lambda b, i: (b, i, 0)),
            pl.BlockSpec((None, W_MIX, W_MIX), lambda b, i: (b, 0, 0)),
        ],
        out_shape=[jax.ShapeDtypeStruct((nseq, L, W_MIX), F32),
                   jax.ShapeDtypeStruct((nseq, W_MIX, W_MIX), F32)],
        scratch_shapes=[pltpu.VMEM((8 + R, 3 * W_MIX), F32), pltpu.VMEM((W_MIX, W_MIX), F32)],
        compiler_params=pltpu.CompilerParams(dimension_semantics=("arbitrary", "arbitrary"),
                                             vmem_limit_bytes=VMEM_LIMIT),
        name="gdn",
    )(qkv, prev8, small, smallt, convw, s0, bd, jnp.asarray(eg), jnp.asarray(eb))


def _ret_kernel(q_ref, k_ref, v_ref, dmat_ref, qdec_ref, kdec_ref, cdec_ref, bd_ref, s0_ref,
                o_ref, sout_ref, sbd):
    step = pl.program_id(1)

    @pl.when(step == 0)
    def _():
        sbd[...] = s0_ref[...]

    q = q_ref[...]
    k = k_ref[...]
    v = v_ref[...]
    s = sbd[...]
    lq = lax.broadcasted_iota(jnp.int32, (1, LANES), 1)
    lv = lax.broadcasted_iota(jnp.int32, (1, W_MIX), 1)
    o = _mm(q * qdec_ref[...], s, HI)
    for h in range(N_HEADS):
        qk = _nt(jnp.where((lq // RET_QK) == h, q, 0.0), k, HI) * dmat_ref[h]
        o = o + jnp.where((lv // HEAD_DIM) == h, _mm(qk, v, HI), 0.0)
    upd = lax.dot_general(k * kdec_ref[...], v, (((0,), (0,)), ((), ())), precision=HI,
                          preferred_element_type=F32)
    sbd[...] = s * cdec_ref[...] + jnp.where(bd_ref[...] > 0.5, upd, 0.0)
    o_ref[...] = o

    @pl.when(step == pl.num_programs(1) - 1)
    def _():
        sout_ref[...] = sbd[...]


def _ret_consts(C, nvalid):
    lg = np.log(1.0 - np.exp2(-5.0 - np.arange(N_HEADS, dtype=np.float64)))
    idx = np.arange(C, dtype=np.float64)
    rel = idx[:, None] - idx[None, :]
    dmat = np.where(rel >= 0, np.exp(np.maximum(rel, 0.0) * lg[:, None, None]), 0.0)
    qd = np.exp((idx + 1.0)[None, :] * lg[:, None])
    kd = np.where(idx[None, :] < nvalid, np.exp((nvalid - 1.0 - idx)[None, :] * lg[:, None]), 0.0)
    cd = np.exp(nvalid * lg)
    qdec = np.repeat(qd.T, RET_QK, axis=1)
    kdec = np.repeat(kd.T, RET_QK, axis=1)
    cdec = np.repeat(cd, HEAD_DIM)[None, :]
    bd = (np.arange(LANES)[:, None] // RET_QK) == (np.arange(W_MIX)[None, :] // HEAD_DIM)
    f = lambda a: jnp.asarray(a.astype(np.float32))
    return f(dmat), f(qdec), f(kdec), f(cdec), f(bd)


def _ret(q, k, v, s0, C, nvalid):
    nseq, L, _ = q.shape
    dmat, qdec, kdec, cdec, bd = _ret_consts(C, nvalid)
    const2 = lambda b, i: (0, 0)
    return pl.pallas_call(
        _ret_kernel,
        grid=(nseq, L // C),
        in_specs=[
            pl.BlockSpec((None, C, LANES), lambda b, i: (b, i, 0)),
            pl.BlockSpec((None, C, LANES), lambda b, i: (b, i, 0)),
            pl.BlockSpec((None, C, W_MIX), lambda b, i: (b, i, 0)),
            pl.BlockSpec((N_HEADS, C, C), lambda b, i: (0, 0, 0)),
            pl.BlockSpec((C, LANES), const2),
            pl.BlockSpec((C, LANES), const2),
            pl.BlockSpec((1, W_MIX), const2),
            pl.BlockSpec((LANES, W_MIX), const2),
            pl.BlockSpec((None, LANES, W_MIX), lambda b, i: (b, 0, 0)),
        ],
        out_specs=[
            pl.BlockSpec((None, C, W_MIX), lambda b, i: (b, i, 0)),
            pl.BlockSpec((None, LANES, W_MIX), lambda b, i: (b, 0, 0)),
        ],
        out_shape=[jax.ShapeDtypeStruct((nseq, L, W_MIX), F32),
                   jax.ShapeDtypeStruct((nseq, LANES, W_MIX), F32)],
        scratch_shapes=[pltpu.VMEM((LANES, W_MIX), F32)],
        compiler_params=pltpu.CompilerParams(dimension_semantics=("arbitrary", "arbitrary"),
                                             vmem_limit_bytes=VMEM_LIMIT),
        name="retention",
    )(q, k, v, dmat, qdec, kdec, cdec, bd, s0)


def _silu(x):
    return x * (1.0 / (1.0 + jnp.exp(-x)))


def _post_kernel(oa_ref, ob_ref, oc_ref, od_ref, gz_ref, rz_ref, x_ref, gains_ref, bd_ref, wout_ref,
                 gffn_ref, wr_ref, br_ref, y_ref, h_ref, gate_ref, *, lam_init):
    bd16 = bd_ref[...].astype(BF16)

    def hnorm(o, gain):
        ms = _mm01_rows(o * o, bd16) * (1.0 / HEAD_DIM)
        return o * lax.rsqrt(ms + RMS_EPS) * gain

    out_a = hnorm(oa_ref[...], gains_ref[0:1, :])
    out_b = hnorm(ob_ref[...], gains_ref[1:2, :]) * _silu(gz_ref[...])
    out_c = hnorm(oc_ref[...], gains_ref[2:3, :]) * _silu(rz_ref[...])
    out_d = hnorm(od_ref[...], gains_ref[3:4, :]) * (1.0 - lam_init)
    y = x_ref[...]
    for k, part in enumerate((out_a, out_b, out_c, out_d)):
        y = y + _mm(part.astype(BF16), wout_ref[k * W_MIX:(k + 1) * W_MIX, :])
    y_ref[...] = y

    ms = jnp.mean(y * y, axis=-1, keepdims=True)
    h = y * lax.rsqrt(ms + RMS_EPS) * gffn_ref[...]
    h_ref[...] = h.astype(BF16)

    lg = _mm(h, wr_ref[...], HI) + br_ref[...]
    lane = lax.broadcasted_iota(jnp.int32, lg.shape, 1)
    lanef = lane.astype(F32)
    is_g = lane < N_GROUPS
    gmax = jnp.max(jnp.where(is_g, lg, NEG), axis=-1, keepdims=True)
    grp = jnp.min(jnp.where(is_g & (lg == gmax), lanef, float(LANES)), axis=-1, keepdims=True).astype(jnp.int32)
    p_sel = 1.0 / jnp.sum(jnp.where(is_g, jnp.exp(lg - gmax), 0.0), axis=-1, keepdims=True)
    e_lane = lane - N_GROUPS
    in_grp = (e_lane >= 0) & (e_lane < N_EXPERTS) & ((e_lane // E_PER_GROUP) == grp)
    vals = jnp.where(in_grp, lg, NEG)
    v1 = jnp.max(vals, axis=-1, keepdims=True)
    i1 = jnp.min(jnp.where(in_grp & (vals == v1), lanef, float(LANES)), axis=-1, keepdims=True).astype(jnp.int32)
    vals2 = jnp.where(lane == i1, NEG, vals)
    v2 = jnp.max(vals2, axis=-1, keepdims=True)
    i2 = jnp.min(jnp.where(in_grp & (lane != i1) & (vals2 == v2), lanef, float(LANES)), axis=-1,
                 keepdims=True).astype(jnp.int32)
    e2 = jnp.exp(v2 - v1)
    w1 = p_sel / (1.0 + e2)
    w2 = p_sel * e2 / (1.0 + e2)
    gate_ref[...] = jnp.where(lane == i1, w1, 0.0) + jnp.where(lane == i2, w2, 0.0)


def _post(oa, ob, oc, od, gz, rz, x2d, gains, wout, gffn, wr, br, lam_init, tm):
    rows = x2d.shape[0]
    hidx = np.arange(W_MIX) // HEAD_DIM
    bd = jnp.asarray((hidx[:, None] == hidx[None, :]).astype(np.float32))
    rowb = lambda w: pl.BlockSpec((tm, w), lambda i: (i, 0))
    const = lambda i: (0, 0)
    return pl.pallas_call(
        functools.partial(_post_kernel, lam_init=lam_init),
        grid=(rows // tm,),
        in_specs=[rowb(W_MIX)] * 6 + [
            rowb(D_MODEL),
            pl.BlockSpec((8, W_MIX), const),
            pl.BlockSpec((W_MIX, W_MIX), const),
            pl.BlockSpec((D_MODEL, D_MODEL), const),
            pl.BlockSpec((1, D_MODEL), const),
            pl.BlockSpec((D_MODEL, LANES), const),
            pl.BlockSpec((1, LANES), const),
        ],
        out_specs=[rowb(D_MODEL), rowb(D_MODEL), rowb(LANES)],
        out_shape=[jax.ShapeDtypeStruct((rows, D_MODEL), F32),
                   jax.ShapeDtypeStruct((rows, D_MODEL), BF16),
                   jax.ShapeDtypeStruct((rows, LANES), F32)],
        compiler_params=pltpu.CompilerParams(dimension_semantics=("arbitrary",),
                                             vmem_limit_bytes=VMEM_LIMIT),
        name="post_mix",
    )(oa, ob, oc, od, gz, rz, x2d, gains, bd, wout, gffn, wr, br)


def _moe_kernel(h_ref, gate_ref, y_ref, wg_ref, wu_ref, wd_ref, gfin_ref, o_ref, *, final_norm):
    e = pl.program_id(1)

    @pl.when(e == 0)
    def _():
        o_ref[...] = y_ref[...]

    h = h_ref[...]
    a = _mm(h, wg_ref[...])
    u = _mm(h, wu_ref[...])
    gates = gate_ref[...]
    lane = lax.broadcasted_iota(jnp.int32, gates.shape, 1)
    ge = jnp.sum(jnp.where(lane == e + N_GROUPS, gates, 0.0), axis=-1, keepdims=True)
    act = _silu(a) * u * ge
    o_ref[...] += _mm(act.astype(BF16), wd_ref[...])

    if final_norm:
        @pl.when(e == N_EXPERTS - 1)
        def _():
            y = o_ref[...]
            ms = jnp.mean(y * y, axis=-1, keepdims=True)
            o_ref[...] = y * lax.rsqrt(ms + RMS_EPS) * gfin_ref[...]


def _moe(h, gates, y, wg, wu, wd, gfin, final_norm, tm):
    rows = h.shape[0]
    return pl.pallas_call(
        functools.partial(_moe_kernel, final_norm=final_norm),
        grid=(rows // tm, N_EXPERTS),
        in_specs=[
            pl.BlockSpec((tm, D_MODEL), lambda i, e: (i, 0)),
            pl.BlockSpec((tm, LANES), lambda i, e: (i, 0)),
            pl.BlockSpec((tm, D_MODEL), lambda i, e: (i, 0)),
            pl.BlockSpec((None, D_MODEL, D_FF), lambda i, e: (e, 0, 0)),
            pl.BlockSpec((None, D_MODEL, D_FF), lambda i, e: (e, 0, 0)),
            pl.BlockSpec((None, D_FF, D_MODEL), lambda i, e: (e, 0, 0)),
            pl.BlockSpec((1, D_MODEL), lambda i, e: (0, 0)),
        ],
        out_specs=pl.BlockSpec((tm, D_MODEL), lambda i, e: (i, 0)),
        out_shape=jax.ShapeDtypeStruct((rows, D_MODEL), F32),
        compiler_params=pltpu.CompilerParams(dimension_semantics=("arbitrary", "arbitrary"),
                                             vmem_limit_bytes=VMEM_LIMIT),
        name="experts",
    )(h, gates, y, wg, wu, wd, gfin)


_SPLIT = (256, 256, 256, 4, 256, 256, 256, 4, 4, 256, 128, 128, 256, 256, 256, 256, 256)


def _arrange_w_in(w):
    idx = np.cumsum(_SPLIT)[:-1].tolist()
    (fq, fk, fv, ff, gq, gk, gv, ga, gb, gz, rq, rk, rv, rz, dq, dk, dv) = jnp.split(w, idx, axis=-1)
    small = jnp.concatenate([ff, ga, gb, jnp.zeros((D_MODEL, LANES - 12), w.dtype)], axis=-1)
    big = jnp.concatenate([fq, fk, fv, gq, gk, gv, gz, rq, rk, rv, rz, dq, dk, dv, small], axis=-1)
    return big.astype(BF16), small[:, :16].T.astype(BF16)


def _pad_lanes(v, n=LANES):
    v = v.reshape(1, -1).astype(F32)
    return jnp.pad(v, ((0, 0), (0, n - v.shape[1])))


def _block_diag_state(s):
    n, hh, dk, dv = s.shape
    eye = jnp.eye(hh, dtype=s.dtype)
    return jnp.einsum('nhkv,hg->nhkgv', s, eye).reshape(n, hh * dk, hh * dv)


def _diag_blocks(sbd, dk, dv):
    n = sbd.shape[0]
    s5 = sbd.reshape(n, N_HEADS, dk, N_HEADS, dv)
    return jnp.stack([s5[:, h, :, h, :] for h in range(N_HEADS)], axis=1)


def _largest_tile(rows, cap):
    t = cap
    while rows % t:
        t //= 2
    return t


def kernel(x_prompt, x_sample, cache_fox_k, cache_fox_v, cache_fox_logf, cache_diff_k, cache_diff_v,
           state_gdn, state_gdn_conv, state_ret, page_table,
           g_mix_norm, w_in, b_fox_f, g_fox_out, w_gdn_conv, gdn_a_log, gdn_dt_bias, g_gdn_out,
           g_ret_out, diff_lambda_q1, diff_lambda_k1, diff_lambda_q2, diff_lambda_k2, g_diff_out,
           w_out, g_ffn_norm, w_router_group, b_router_group, w_router_expert, b_router_expert,
           w_exp_gate, w_exp_up, w_exp_down, g_final):
    depth = w_in.shape[0]
    _, S, _ = x_prompt.shape
    N, L, _ = x_sample.shape
    n_pool = cache_fox_k.shape[1]
    assert x_prompt.shape[0] == 1 and L == 4 and S % 512 == 0

    yp = x_prompt.reshape(S, D_MODEL)
    ys = x_sample.reshape(N * L, D_MODEL)
    slopes = jnp.exp2(-8.0 * jnp.arange(1, N_HEADS + 1, dtype=F32) / N_HEADS)
    gfin = g_final.reshape(1, D_MODEL)
    T_ATT = 512
    tm_p = _largest_tile(S, 512)
    tm_moe = _largest_tile(S, 1024)

    def slab_view(c):
        return c.transpose(0, 1, 3, 4, 2).reshape(depth, n_pool, W_MIX, PAGE)

    fk_t, fv_t = slab_view(cache_fox_k), slab_view(cache_fox_v)
    dk_t, dv_t = slab_view(cache_diff_k), slab_view(cache_diff_v)

    ent_p, ent_s = [], []
    for l in range(depth):
        lam_init = 0.8 - 0.6 * math.exp(-0.3 * l)
        last = l == depth - 1
        w_arr, wst = _arrange_w_in(w_in[l])
        brow = _pad_lanes(jnp.concatenate([b_fox_f[l], gdn_dt_bias[l]]))
        arow = _pad_lanes(jnp.concatenate([jnp.zeros((4,), F32), gdn_a_log[l]]))
        bcol = jnp.broadcast_to(brow[0, :16, None], (16, LANES))
        acol = jnp.broadcast_to(arow[0, :16, None], (16, LANES))
        gmix = g_mix_norm[l].reshape(1, D_MODEL)
        lam_params = jnp.pad(jnp.stack([diff_lambda_q1[l], diff_lambda_k1[l],
                                        diff_lambda_q2[l], diff_lambda_k2[l]]).astype(F32),
                             ((0, 4), (0, LANES - DIFF_QK)))
        convw = jnp.pad(w_gdn_conv[l].astype(F32), ((0, 8 - GDN_CONV), (0, 0)))
        gains = jnp.pad(jnp.stack([g_fox_out[l].reshape(-1), jnp.tile(g_gdn_out[l], N_HEADS),
                                   g_ret_out[l].reshape(-1), jnp.tile(g_diff_out[l], N_HEADS)]).astype(F32),
                        ((0, 4), (0, 0)))
        wout = w_out[l].astype(BF16)
        gffn = g_ffn_norm[l].reshape(1, D_MODEL)
        wr = jnp.pad(jnp.concatenate([w_router_group[l], w_router_expert[l].reshape(D_MODEL, N_EXPERTS)],
                                     axis=-1).astype(F32), ((0, 0), (0, LANES - N_GROUPS - N_EXPERTS)))
        br = _pad_lanes(jnp.concatenate([b_router_group[l], b_router_expert[l].reshape(-1)]))
        wg = w_exp_gate[l].astype(BF16)
        wu = w_exp_up[l].astype(BF16)
        wd = w_exp_down[l].astype(BF16)

        zp = _inproj(yp, gmix, w_arr, wst, brow, arow, bcol, acol, tm_p)
        o_a = _fox_prompt(zp[I_FQ], zp[I_FKB], zp[I_FVB], zp[I_CT][0:4], T_ATT)
        o_d = _diff_prompt(zp[I_DQ], zp[I_DKB], zp[I_DVB], slopes, lam_params, lam_init, T_ATT)
        o_b, sb_p = _gdn(zp[I_GQKV][None], jnp.zeros((1, 8, 3 * W_MIX), F32), zp[I_SMALL][None],
                         zp[I_SMALLT][None], convw, jnp.zeros((1, W_MIX, W_MIX), F32),
                         GDN_CHUNK, 2, GDN_CHUNK)
        o_c, sc_p = _ret(zp[I_RQ][None], zp[I_RK][None], zp[I_RV][None],
                         jnp.zeros((1, LANES, W_MIX), F32), RET_CHUNK, RET_CHUNK)
        y1, h2, gates = _post(o_a, o_b[0], o_c[0], o_d, zp[I_GZ], zp[I_RZ], yp, gains, wout, gffn,
                              wr, br, lam_init, tm_p)
        yp = _moe(h2, gates, y1, wg, wu, wd, gfin, last, tm_moe)
        ent_p.append((
            zp[I_FK].reshape(1, S, N_HEADS, HEAD_DIM), zp[I_FV].reshape(1, S, N_HEADS, HEAD_DIM),
            zp[I_SMALLT][0:4].T.reshape(1, S, N_HEADS),
            zp[I_DK].reshape(1, S, N_HEADS, HEAD_DIM), zp[I_DV].reshape(1, S, N_HEADS, HEAD_DIM),
            _diag_blocks(sb_p, HEAD_DIM, HEAD_DIM),
            zp[I_GQKV][S - (GDN_CONV - 1):].reshape(1, GDN_CONV - 1, 3 * W_MIX),
            _diag_blocks(sc_p, RET_QK, HEAD_DIM)))

        R = N * L
        zs = _inproj(ys, gmix, w_arr, wst, brow, arow, bcol, acol, R)

        def seq8(a):
            return jnp.pad(a.astype(F32).reshape(N, L, -1), ((0, 0), (0, 8 - L), (0, 0)))

        def seq_page_t(a):
            return jnp.pad(a.reshape(N, L, -1), ((0, 0), (0, PAGE - L), (0, 0))).transpose(0, 2, 1)

        logf_new = zs[I_SMALLT][0:4].reshape(4, N, L).transpose(1, 0, 2)
        flogn = jnp.pad(logf_new, ((0, 0), (0, 4), (0, PAGE - L)))
        o_af, o_df = _decode(
            page_table, l, seq8(zs[I_FQ]), seq8(zs[I_DQ]), fk_t, fv_t,
            jnp.pad(cache_fox_logf[l].transpose(0, 2, 1), ((0, 0), (0, 4), (0, 0))), dk_t, dv_t,
            seq_page_t(zs[I_FKB]), seq_page_t(zs[I_FVB]), flogn, seq_page_t(zs[I_DKB]),
            seq_page_t(zs[I_DVB]), lam_params, lam_init)
        o_a_s = o_af[:, :L].reshape(R, W_MIX)
        o_d_s = o_df[:, :L].reshape(R, W_MIX)

        def seq_rows(a, rows):
            return jnp.pad(a.reshape(N, L, -1), ((0, 0), (0, rows - L), (0, 0)))

        prev8 = jnp.pad(state_gdn_conv[l].astype(F32), ((0, 0), (8 - (GDN_CONV - 1), 0), (0, 0)))
        smallt_s = jnp.pad(zs[I_SMALLT].reshape(16, N, L).transpose(1, 0, 2),
                           ((0, 0), (0, 0), (0, LANES - L)))
        o_bs, sb_s = _gdn(seq_rows(zs[I_GQKV], GDN_CHUNK), prev8, seq_rows(zs[I_SMALL], GDN_CHUNK),
                          smallt_s, convw, _block_diag_state(state_gdn[l].astype(F32)),
                          GDN_CHUNK, 1, L)
        o_cs, sc_s = _ret(seq_rows(zs[I_RQ], RET_CHUNK), seq_rows(zs[I_RK], RET_CHUNK),
                          seq_rows(zs[I_RV], RET_CHUNK), _block_diag_state(state_ret[l].astype(F32)),
                          RET_CHUNK, L)
        o_b_s = o_bs[:, :L].reshape(R, W_MIX)
        o_c_s = o_cs[:, :L].reshape(R, W_MIX)
        y1s, h2s, gates_s = _post(o_a_s, o_b_s, o_c_s, o_d_s, zs[I_GZ], zs[I_RZ], ys, gains, wout,
                                  gffn, wr, br, lam_init, R)
        ys = _moe(h2s, gates_s, y1s, wg, wu, wd, gfin, last, R)
        gq = zs[I_GQKV].reshape(N, L, 3 * W_MIX)
        conv_new = jnp.concatenate([state_gdn_conv[l].astype(F32), gq], axis=1)[:, L:]
        ent_s.append((
            zs[I_FK].reshape(N, L, N_HEADS, HEAD_DIM), zs[I_FV].reshape(N, L, N_HEADS, HEAD_DIM),
            logf_new.transpose(0, 2, 1),
            zs[I_DK].reshape(N, L, N_HEADS, HEAD_DIM), zs[I_DV].reshape(N, L, N_HEADS, HEAD_DIM),
            _diag_blocks(sb_s, HEAD_DIM, HEAD_DIM), conv_new, _diag_blocks(sc_s, RET_QK, HEAD_DIM)))

    stack = lambda ents, i: jnp.stack([e[i] for e in ents])
    return ((yp.reshape(1, S, D_MODEL), ys.reshape(N, L, D_MODEL))
            + tuple(stack(ent_p, i) for i in range(8))
            + tuple(stack(ent_s, i) for i in range(8)))
```

```python
import functools
import math

import numpy as np
import jax
import jax.numpy as jnp
from jax import lax
from jax.experimental import pallas as pl
from jax.experimental.pallas import tpu as pltpu

F32 = jnp.float32
BF16 = jnp.bfloat16
HI = lax.Precision.HIGHEST

D_MODEL = 1024
HEAD_DIM = 64
N_HEADS = 4
W_MIX = N_HEADS * HEAD_DIM
RET_QK = 32
DIFF_QK = 32
GDN_CONV = 4
GDN_CHUNK = 64
RET_CHUNK = 128
PAGE = 128
N_GROUPS = 4
E_PER_GROUP = 4
N_EXPERTS = 16
D_FF = 256
RMS_EPS = 1e-6
LOG2E = 1.4426950408889634
NEG = -1e30
LANES = 128
VMEM_LIMIT = 48 * 1024 * 1024

OFF_FQ, OFF_FK, OFF_FV = 0, 256, 512
OFF_GQKV, OFF_GZ = 768, 1536
OFF_RQ, OFF_RK, OFF_RV, OFF_RZ = 1792, 1920, 2048, 2304
OFF_DQ, OFF_DK, OFF_DV = 2560, 2816, 3072
OFF_SMALL = 3328
W_IN_COLS = 3456

SC_FOX = HEAD_DIM ** -0.5 * LOG2E
SC_DIFF = DIFF_QK ** -0.5 * LOG2E

_PIECES = (
    (OFF_FQ, 256, ((SC_FOX, BF16),)),
    (OFF_FK, 256, ((1.0, F32), (1.0, BF16))),
    (OFF_FV, 256, ((1.0, F32), (1.0, BF16))),
    (OFF_GQKV, 768, ((1.0, F32),)),
    (OFF_GZ, 256, ((1.0, F32),)),
    (OFF_RQ, 128, ((1.0, F32),)),
    (OFF_RK, 128, ((RET_QK ** -0.5, F32),)),
    (OFF_RV, 256, ((1.0, F32),)),
    (OFF_RZ, 256, ((1.0, F32),)),
    (OFF_DQ, 256, ((SC_DIFF, BF16),)),
    (OFF_DK, 256, ((1.0, F32), (1.0, BF16))),
    (OFF_DV, 256, ((1.0, F32), (1.0, BF16))),
)
(I_FQ, I_FK, I_FKB, I_FV, I_FVB, I_GQKV, I_GZ, I_RQ, I_RK, I_RV, I_RZ,
 I_DQ, I_DK, I_DKB, I_DV, I_DVB, I_SMALL, I_SMALLT, I_CT) = range(19)


def _nt(a, b, precision=None):
    return lax.dot_general(a, b, (((1,), (1,)), ((), ())), precision=precision,
                           preferred_element_type=F32)


def _mm(a, b, precision=None):
    return jnp.dot(a, b, precision=precision, preferred_element_type=F32)


def _gate_values(z, kind, a_log):
    t = jnp.log1p(jnp.exp(-jnp.abs(z)))
    logf = jnp.minimum(z, 0.0) - t
    g = -jnp.exp(a_log) * (jnp.maximum(z, 0.0) + t)
    beta = 1.0 / (1.0 + jnp.exp(-z))
    return jnp.where(kind == 0, logf, jnp.where(kind == 1, g, jnp.where(kind == 2, beta, 0.0)))


def _inproj_kernel(x_ref, g_ref, w_ref, wst_ref, brow_ref, arow_ref, bcol_ref, acol_ref, tri_ref,
                   *refs):
    outs = refs[:19]
    carry_ref = refs[19]
    i = pl.program_id(0)
    tm = x_ref.shape[0]

    @pl.when(i == 0)
    def _():
        carry_ref[...] = jnp.zeros_like(carry_ref)

    x = x_ref[...]
    ms = jnp.mean(x * x, axis=-1, keepdims=True)
    h = (x * lax.rsqrt(ms + RMS_EPS) * g_ref[...]).astype(BF16)

    k = 0
    for off, wd, outs_spec in _PIECES:
        z = _mm(h, w_ref[:, off:off + wd])
        for scale, _ in outs_spec:
            zz = z if scale == 1.0 else z * scale
            outs[k][...] = zz.astype(outs[k].dtype)
            k += 1

    zs = _mm(h, w_ref[:, OFF_SMALL:OFF_SMALL + LANES]) + brow_ref[...]
    kind = lax.broadcasted_iota(jnp.int32, zs.shape, 1) // 4
    outs[I_SMALL][...] = _gate_values(zs, kind, arow_ref[...])

    zt = _nt(wst_ref[...], h) + bcol_ref[:, 0:1]
    kind_t = lax.broadcasted_iota(jnp.int32, zt.shape, 0) // 4
    vt = _gate_values(zt, kind_t, acol_ref[:, 0:1])
    outs[I_SMALLT][...] = vt
    ct = _mm(vt, tri_ref[...], HI) + carry_ref[:, 0:1]
    outs[I_CT][...] = ct
    carry_ref[...] = jnp.broadcast_to(ct[:, tm - 1:tm], carry_ref.shape)


def _inproj(x2d, gain, w_arr, wst, brow, arow, bcol, acol, tm):
    rows = x2d.shape[0]
    tri = jnp.triu(jnp.ones((tm, tm), F32))
    shapes, specs = [], []
    for _, wd, outs_spec in _PIECES:
        for _, dt in outs_spec:
            shapes.append(jax.ShapeDtypeStruct((rows, wd), dt))
            specs.append(pl.BlockSpec((tm, wd), lambda i: (i, 0)))
    shapes.append(jax.ShapeDtypeStruct((rows, LANES), F32))
    specs.append(pl.BlockSpec((tm, LANES), lambda i: (i, 0)))
    for _ in range(2):
        shapes.append(jax.ShapeDtypeStruct((16, rows), F32))
        specs.append(pl.BlockSpec((16, tm), lambda i: (0, i)))
    const = lambda i: (0, 0)
    return pl.pallas_call(
        _inproj_kernel,
        grid=(rows // tm,),
        in_specs=[
            pl.BlockSpec((tm, D_MODEL), lambda i: (i, 0)),
            pl.BlockSpec((1, D_MODEL), const),
            pl.BlockSpec((D_MODEL, W_IN_COLS), const),
            pl.BlockSpec((16, D_MODEL), const),
            pl.BlockSpec((1, LANES), const),
            pl.BlockSpec((1, LANES), const),
            pl.BlockSpec((16, LANES), const),
            pl.BlockSpec((16, LANES), const),
            pl.BlockSpec((tm, tm), const),
        ],
        out_specs=specs,
        out_shape=shapes,
        scratch_shapes=[pltpu.VMEM((16, LANES), F32)],
        compiler_params=pltpu.CompilerParams(dimension_semantics=("arbitrary",),
                                             vmem_limit_bytes=VMEM_LIMIT),
        name="inproj",
    )(x2d, gain, w_arr, wst, brow, arow, bcol, acol, tri)


def _softmax_step(s, m, acc, v):
    m_new = jnp.maximum(m, jnp.max(s, axis=-1, keepdims=True))
    alpha = jnp.exp2(m - m_new)
    p = jnp.exp2(s - m_new)
    acc = alpha * acc + _mm(p.astype(BF16), v)
    return m_new, acc


def _finish_pair(o, sel, half, o_ref):
    @pl.when(half == 0)
    def _():
        o_ref[...] = jnp.where(sel, o, 0.0)

    @pl.when(half == 1)
    def _():
        o_ref[...] = jnp.where(sel, o, o_ref[...])


def _causal_tiles(step, init, qi, r):
    carry = lax.fori_loop(0, qi * r, lambda j, c: step(j, c, None), init)
    for d in range(r):
        carry = step(qi * r + d, carry, d)
    return carry


def _fox_kernel(q_ref, k_ref, v_ref, c_ref, o_ref, *, TQ, TK):
    qi = pl.program_id(1)
    half = pl.program_id(2)
    r = TQ // TK
    lane = lax.broadcasted_iota(jnp.int32, (1, LANES), 1)
    sel = (lane // HEAD_DIM) == half
    q = q_ref[...]
    q = jnp.where(sel, q, jnp.zeros_like(q))
    c0 = c_ref[qi * r][:, 0:1]
    row = lax.broadcasted_iota(jnp.int32, (TQ, TK), 0)
    col = lax.broadcasted_iota(jnp.int32, (TQ, TK), 1)

    def step(j, carry, diag):
        off = pl.multiple_of(j * TK, TK)
        kj = k_ref[pl.ds(off, TK), :]
        vj = v_ref[pl.ds(off, TK), :]
        vj = jnp.where(sel, vj, jnp.ones_like(vj))
        s = _nt(q, kj) + (c0 - c_ref[j]) * LOG2E
        if diag is not None:
            s = jnp.where(col + diag * TK <= row, s, NEG)
        return _softmax_step(s, *carry, vj)

    init = (jnp.full((TQ, 1), NEG, F32), jnp.zeros((TQ, LANES), F32))
    _, acc = _causal_tiles(step, init, qi, r)
    l = jnp.max(jnp.where(sel, 0.0, acc), axis=-1, keepdims=True)
    _finish_pair(acc * (1.0 / l), sel, half, o_ref)


def _fox_prompt(qb, kb, vb, c4, TQ, TK):
    S = qb.shape[0]
    nk = S // TK
    cc = c4.reshape(N_HEADS, nk, 1, TK)
    return pl.pallas_call(
        functools.partial(_fox_kernel, TQ=TQ, TK=TK),
        grid=(2, S // TQ, 2),
        in_specs=[
            pl.BlockSpec((TQ, LANES), lambda p, i, h: (i, p)),
            pl.BlockSpec((S, LANES), lambda p, i, h: (0, p)),
            pl.BlockSpec((S, LANES), lambda p, i, h: (0, p)),
            pl.BlockSpec((None, nk, 1, TK), lambda p, i, h: (2 * p + h, 0, 0, 0)),
        ],
        out_specs=pl.BlockSpec((TQ, LANES), lambda p, i, h: (i, p)),
        out_shape=jax.ShapeDtypeStruct((S, W_MIX), F32),
        compiler_params=pltpu.CompilerParams(
            dimension_semantics=("arbitrary", "arbitrary", "arbitrary"),
            vmem_limit_bytes=VMEM_LIMIT),
        name="fox_prompt",
    )(qb, kb, vb, cc)


def _diff_lambda(lp_ref, lam_init):
    lp = lp_ref[...]
    a = jnp.sum(lp[0:1] * lp[1:2], axis=-1, keepdims=True)
    b = jnp.sum(lp[2:3] * lp[3:4], axis=-1, keepdims=True)
    return jnp.exp(a) - jnp.exp(b) + lam_init


def _diff_kernel(slopes_ref, q_ref, k_ref, v_ref, lp_ref, o_ref, *, TQ, TK, lam_init):
    pair = pl.program_id(0)
    qi = pl.program_id(1)
    half = pl.program_id(2)
    r = TQ // TK
    lane = lax.broadcasted_iota(jnp.int32, (1, LANES), 1)
    sel = (lane // HEAD_DIM) == half
    sel1 = (lane // DIFF_QK) == 2 * half
    sel2 = (lane // DIFF_QK) == 2 * half + 1
    q = q_ref[...]
    zero = jnp.zeros_like(q)
    q12 = jnp.concatenate([jnp.where(sel1, q, zero), jnp.where(sel2, q, zero)], axis=0)
    slope = slopes_ref[2 * pair + half] * LOG2E
    row = lax.broadcasted_iota(jnp.int32, (2 * TQ, TK), 0) % TQ
    col = lax.broadcasted_iota(jnp.int32, (2 * TQ, TK), 1)
    kcol = lax.broadcasted_iota(jnp.int32, (1, TK), 1)

    def step(j, carry, diag):
        off = pl.multiple_of(j * TK, TK)
        kj = k_ref[pl.ds(off, TK), :]
        vj = v_ref[pl.ds(off, TK), :]
        vj = jnp.where(sel, vj, jnp.ones_like(vj))
        s = _nt(q12, kj) + slope * ((j - qi * r) * TK + kcol).astype(F32)
        if diag is not None:
            s = jnp.where(col + diag * TK <= row, s, NEG)
        return _softmax_step(s, *carry, vj)

    init = (jnp.full((2 * TQ, 1), NEG, F32), jnp.zeros((2 * TQ, LANES), F32))
    _, acc = _causal_tiles(step, init, qi, r)
    l = jnp.max(jnp.where(sel, 0.0, acc), axis=-1, keepdims=True)
    o = acc * (1.0 / l)
    lam = _diff_lambda(lp_ref, lam_init)
    _finish_pair(o[:TQ] - lam * o[TQ:], sel, half, o_ref)


def _diff_prompt(qb, kb, vb, slopes, lam_params, lam_init, TQ, TK):
    S = qb.shape[0]
    return pl.pallas_call(
        functools.partial(_diff_kernel, TQ=TQ, TK=TK, lam_init=lam_init),
        grid=(2, S // TQ, 2),
        in_specs=[
            pl.BlockSpec(memory_space=pltpu.SMEM),
            pl.BlockSpec((TQ, LANES), lambda p, i, h: (i, p)),
            pl.BlockSpec((S, LANES), lambda p, i, h: (0, p)),
            pl.BlockSpec((S, LANES), lambda p, i, h: (0, p)),
            pl.BlockSpec((8, LANES), lambda p, i, h: (0, 0)),
        ],
        out_specs=pl.BlockSpec((TQ, LANES), lambda p, i, h: (i, p)),
        out_shape=jax.ShapeDtypeStruct((S, W_MIX), F32),
        compiler_params=pltpu.CompilerParams(
            dimension_semantics=("arbitrary", "arbitrary", "arbitrary"),
            vmem_limit_bytes=VMEM_LIMIT),
        name="diff_prompt",
    )(slopes, qb, kb, vb, lam_params)


def _pick_heads(acc32):
    lane = lax.broadcasted_iota(jnp.int32, (8, W_MIX), 1)
    out = jnp.zeros((8, W_MIX), F32)
    for h in range(N_HEADS):
        out = jnp.where((lane // HEAD_DIM) == h, acc32[8 * h:8 * h + 8], out)
    return out


def _split3(x):
    hi = x.astype(BF16).astype(F32)
    r1 = x - hi
    mid = r1.astype(BF16).astype(F32)
    return hi, mid, r1 - mid


def _decode_kernel(pt_ref, fq_ref, dq_ref, *refs, P, n_steps, n_pages, lam_init):
    fk, fv, fl, dk, dv = (refs[i * P:(i + 1) * P] for i in range(5))
    fkn_ref, fvn_ref, flogn_ref, dkn_ref, dvn_ref, tri_ref, lp_ref, of_ref, od_ref = refs[5 * P:5 * P + 9]
    qf_s, qd_s, mf_s, lf_s, af_s, md_s, ld_s, ad_s, cc_s = refs[5 * P + 9:]
    j = pl.program_id(1)
    lane8 = lax.broadcasted_iota(jnp.int32, (8, W_MIX), 1)

    @pl.when(j == 0)
    def _():
        fq = fq_ref[...]
        dq = dq_ref[...]
        qf_s[...] = jnp.concatenate(
            [jnp.where((lane8 // HEAD_DIM) == h, fq, 0.0) for h in range(N_HEADS)], axis=0).astype(BF16)
        d1 = [jnp.where((lane8 // DIFF_QK) == 2 * h, dq, 0.0) for h in range(N_HEADS)]
        d2 = [jnp.where((lane8 // DIFF_QK) == 2 * h + 1, dq, 0.0) for h in range(N_HEADS)]
        qd_s[...] = jnp.concatenate(d1 + d2, axis=0).astype(BF16)
        mf_s[...] = jnp.full(mf_s.shape, NEG, F32)
        lf_s[...] = jnp.zeros_like(lf_s)
        af_s[...] = jnp.zeros_like(af_s)
        md_s[...] = jnp.full(md_s.shape, NEG, F32)
        ld_s[...] = jnp.zeros_like(ld_s)
        ad_s[...] = jnp.zeros_like(ad_s)
        cc_s[...] = jnp.zeros_like(cc_s)

    tri = tri_ref[...]
    r64 = lax.broadcasted_iota(jnp.int32, (64, 1), 0)
    hd = (r64 % 32) // 8
    slope64 = jnp.exp2(-2.0 * (hd + 1).astype(F32)) * LOG2E
    kcol = lax.broadcasted_iota(jnp.int32, (1, PAGE), 1)
    zero8 = jnp.zeros((8, PAGE), F32)

    def expand_heads(c8):
        return jnp.concatenate([jnp.broadcast_to(c8[h:h + 1], (8, PAGE)) for h in range(N_HEADS)], axis=0)

    def cumsum_lanes(xs):
        parts = []
        for x in xs:
            parts += list(_split3(x)) + [zero8]
        loc = _mm(jnp.concatenate(parts, axis=0).astype(BF16), tri)
        return [loc[32 * i:32 * i + 8] + loc[32 * i + 8:32 * i + 16] + loc[32 * i + 16:32 * i + 24]
                for i in range(len(xs))]

    def update(s, vts, m_s, l_s, a_s):
        m_old = m_s[...]
        m_new = jnp.maximum(m_old, jnp.max(s, axis=-1, keepdims=True))
        alpha = jnp.exp2(m_old - m_new)
        p = jnp.exp2(s - m_new)
        l_s[...] = alpha * l_s[...] + jnp.sum(p, axis=-1, keepdims=True)
        acc = alpha * a_s[...]
        for i, vt in enumerate(vts):
            acc = acc + _nt(p[:, i * PAGE:(i + 1) * PAGE].astype(BF16), vt)
        m_s[...] = m_new
        a_s[...] = acc

    locs = cumsum_lanes([fl[i][...] for i in range(P)])
    carry = cc_s[:, 0:1]
    sf, sd = [], []
    for i in range(P):
        cum = locs[i] + carry
        carry = cum[:, PAGE - 1:PAGE]
        sf.append(_mm(qf_s[...], fk[i][...].astype(BF16)) - expand_heads(cum) * LOG2E)
        pos = ((j * P + i) - n_pages) * PAGE + kcol
        sd.append(_mm(qd_s[...], dk[i][...].astype(BF16)) + slope64 * pos.astype(F32))
    cc_s[...] = jnp.broadcast_to(carry, cc_s.shape)
    update(jnp.concatenate(sf, axis=1), [fv[i][...].astype(BF16) for i in range(P)], mf_s, lf_s, af_s)
    update(jnp.concatenate(sd, axis=1), [dv[i][...].astype(BF16) for i in range(P)], md_s, ld_s, ad_s)

    @pl.when(j == n_steps - 1)
    def _():
        cum_n = cumsum_lanes([flogn_ref[...]])[0] + cc_s[:, 0:1]
        r32 = lax.broadcasted_iota(jnp.int32, (32, PAGE), 0)
        c32 = lax.broadcasted_iota(jnp.int32, (32, PAGE), 1)
        sfn = _mm(qf_s[...], fkn_ref[...]) - expand_heads(cum_n) * LOG2E
        update(jnp.where(c32 <= (r32 % 8), sfn, NEG), [fvn_ref[...]], mf_s, lf_s, af_s)
        r = lax.broadcasted_iota(jnp.int32, (64, PAGE), 0)
        c = lax.broadcasted_iota(jnp.int32, (64, PAGE), 1)
        sdn = _mm(qd_s[...], dkn_ref[...]) + slope64 * kcol.astype(F32)
        update(jnp.where(c <= (r % 8), sdn, NEG), [dvn_ref[...]], md_s, ld_s, ad_s)

        of_ref[...] = _pick_heads(af_s[...] * (1.0 / lf_s[...]))
        od = ad_s[...] * (1.0 / ld_s[...])
        lam = _diff_lambda(lp_ref, lam_init)
        od_ref[...] = _pick_heads(od[0:32] - lam * od[32:64])


def _decode(page_table, layer, fq8, dq8, fk_t, fv_t, flog_t, dk_t, dv_t,
            fkn, fvn, flogn, dkn, dvn, lam_params, lam_init):
    n, n_pages = page_table.shape
    P = 16
    while n_pages % P:
        P //= 2
    n_steps = n_pages // P
    tri = jnp.triu(jnp.ones((PAGE, PAGE), BF16))
    seq = lambda b, j, pt: (b, 0, 0)
    const = lambda b, j, pt: (0, 0)

    def cache_spec(i):
        return pl.BlockSpec((None, None, W_MIX, PAGE), lambda b, j, pt: (layer, pt[b, j * P + i], 0, 0))

    def flog_spec(i):
        return pl.BlockSpec((None, 8, PAGE), lambda b, j, pt: (pt[b, j * P + i], 0, 0))

    pages = lambda: [cache_spec(i) for i in range(P)]
    grid_spec = pltpu.PrefetchScalarGridSpec(
        num_scalar_prefetch=1,
        grid=(n, n_steps),
        in_specs=[pl.BlockSpec((None, 8, W_MIX), seq), pl.BlockSpec((None, 8, W_MIX), seq)]
        + pages() + pages() + [flog_spec(i) for i in range(P)] + pages() + pages() + [
            pl.BlockSpec((None, W_MIX, PAGE), seq),
            pl.BlockSpec((None, W_MIX, PAGE), seq),
            pl.BlockSpec((None, 8, PAGE), seq),
            pl.BlockSpec((None, W_MIX, PAGE), seq),
            pl.BlockSpec((None, W_MIX, PAGE), seq),
            pl.BlockSpec((PAGE, PAGE), const),
            pl.BlockSpec((8, LANES), const),
        ],
        out_specs=[pl.BlockSpec((None, 8, W_MIX), seq), pl.BlockSpec((None, 8, W_MIX), seq)],
        scratch_shapes=[
            pltpu.VMEM((32, W_MIX), BF16), pltpu.VMEM((64, W_MIX), BF16),
            pltpu.VMEM((32, 1), F32), pltpu.VMEM((32, 1), F32), pltpu.VMEM((32, W_MIX), F32),
            pltpu.VMEM((64, 1), F32), pltpu.VMEM((64, 1), F32), pltpu.VMEM((64, W_MIX), F32),
            pltpu.VMEM((8, LANES), F32),
        ],
    )
    return pl.pallas_call(
        functools.partial(_decode_kernel, P=P, n_steps=n_steps, n_pages=n_pages, lam_init=lam_init),
        grid_spec=grid_spec,
        out_shape=[jax.ShapeDtypeStruct((n, 8, W_MIX), F32), jax.ShapeDtypeStruct((n, 8, W_MIX), F32)],
        compiler_params=pltpu.CompilerParams(dimension_semantics=("arbitrary", "arbitrary"),
                                             vmem_limit_bytes=VMEM_LIMIT),
        name="paged_decode",
    )(page_table, fq8, dq8, *([fk_t] * P), *([fv_t] * P), *([flog_t] * P), *([dk_t] * P), *([dv_t] * P),
      fkn, fvn, flogn, dkn, dvn, tri, lam_params)


def _mm_b(a, b):
    return _mm(a.astype(BF16), b.astype(BF16))


def _nt_b(a, b):
    return _nt(a.astype(BF16), b.astype(BF16))


def _mm_x2(a, b):
    ah = a.astype(BF16)
    al = (a - ah.astype(F32)).astype(BF16)
    bh = b.astype(BF16)
    bl = (b - bh.astype(F32)).astype(BF16)
    return _mm(ah, bh) + (_mm(ah, bl) + _mm(al, bh))


def _mm01_rows(x, b01):
    m = x.shape[0]
    r = _mm(jnp.concatenate(_split3(x), axis=0).astype(BF16), b01)
    return r[0:m] + r[m:2 * m] + r[2 * m:3 * m]


def _mm01_cols(a01, x):
    n = x.shape[1]
    r = _mm(a01, jnp.concatenate(_split3(x), axis=1).astype(BF16))
    return r[:, 0:n] + r[:, n:2 * n] + r[:, 2 * n:3 * n]


def _split2(x):
    h = x.astype(BF16)
    return h, (x - h.astype(F32)).astype(BF16)


def _mm_ss(a, b):
    return _mm(a[0], b[0]) + (_mm(a[0], b[1]) + _mm(a[1], b[0]))


def _inv_unit_lower(a_list, C):
    row = lax.broadcasted_iota(jnp.int32, (C, C), 0)
    col = lax.broadcasted_iota(jnp.int32, (C, C), 1)
    eye = (row == col).astype(F32)
    leaf = (row // 8) == (col // 8)
    n1 = [jnp.where(leaf, a, 0.0) for a in a_list]
    n1s = [_split2(x) for x in n1]
    n2s = [_split2(_mm_ss(s, s)) for s in n1s]
    n4s = [_split2(_mm_ss(s, s)) for s in n2s]
    d = [eye - x for x in n1]
    d = [x + _mm_ss(_split2(x), s) for x, s in zip(d, n2s)]
    d = [x + _mm_ss(_split2(x), s) for x, s in zip(d, n4s)]
    b = 8
    while b < C:
        m = ((row // (2 * b)) == (col // (2 * b))) & ((row // b) != (col // b))
        offs = [_split2(jnp.where(m, a, 0.0)) for a in a_list]
        ds = [_split2(x) for x in d]
        ts = [_split2(_mm_ss(o, s)) for o, s in zip(offs, ds)]
        d = [x - _mm_ss(s, t) for x, s, t in zip(d, ds, ts)]
        b *= 2
    return d


def _gdn_kernel(qkv_ref, prev_ref, small_ref, smallt_ref, convw_ref, s0_ref, bd_ref, eg_ref, eb_ref,
                o_ref, sout_ref, xbuf, sbd, *, C, CPS, nvalid):
    step = pl.program_id(1)
    R = C * CPS

    @pl.when(step == 0)
    def _():
        xbuf[0:8, :] = prev_ref[...]
        sbd[...] = s0_ref[...]

    xbuf[8:8 + R, :] = qkv_ref[...]
    y = xbuf[5:5 + R, :] * convw_ref[0:1, :]
    for t in range(1, GDN_CONV):
        y = y + xbuf[5 + t:5 + t + R, :] * convw_ref[t:t + 1, :]
    y = y * (1.0 / (1.0 + jnp.exp(-y)))
    tail = xbuf[R:R + 8, :]
    xbuf[0:8, :] = tail

    bd = bd_ref[...]
    bdmask = bd > 0.5
    bd16 = bd.astype(BF16)
    egb16 = jnp.concatenate([eg_ref[...], eb_ref[...]], axis=1).astype(BF16)
    row = lax.broadcasted_iota(jnp.int32, (C, C), 0)
    col = lax.broadcasted_iota(jnp.int32, (C, C), 1)
    incl = row >= col
    strict = row > col
    ltri = incl.astype(F32).astype(BF16)
    utri = (row <= col).astype(F32).astype(BF16)
    lane = lax.broadcasted_iota(jnp.int32, (1, W_MIX), 1)
    rvalid = lax.broadcasted_iota(jnp.int32, (C, 1), 0) < nvalid
    cvalid = lax.broadcasted_iota(jnp.int32, (1, C), 1) < nvalid

    heads = [(lane // HEAD_DIM) == h for h in range(N_HEADS)]

    chunks = []
    for c in range(CPS):
        r0 = c * C
        xq = y[r0:r0 + C, 0:W_MIX]
        xk = y[r0:r0 + C, W_MIX:2 * W_MIX]
        xv = y[r0:r0 + C, 2 * W_MIX:3 * W_MIX]
        sm = small_ref[r0:r0 + C, :]
        smt = smallt_ref[:, r0:r0 + C]
        if nvalid < C:
            xq = jnp.where(rvalid, xq, 0.0)
            xk = jnp.where(rvalid, xk, 0.0)
            xv = jnp.where(rvalid, xv, 0.0)
            sm = jnp.where(rvalid, sm, 0.0)
            smt = jnp.where(cvalid, smt, 0.0)
        qn = xq * lax.rsqrt(_mm01_rows(xq * xq, bd16) + 1e-6) * HEAD_DIM ** -0.5
        kn = xk * lax.rsqrt(_mm01_rows(xk * xk, bd16) + 1e-6)
        gb = _mm01_rows(sm, egb16)
        gexp = gb[:, 0:W_MIX]
        bexp = gb[:, W_MIX:2 * W_MIX]
        gcum = _mm01_cols(ltri, gexp)
        grow = _mm01_rows(smt, utri)
        eg = jnp.exp(gcum)
        glast = gcum[C - 1:C, :]
        chunks.append(dict(qn=qn, kn=kn, bexp=bexp, gcum=gcum, grow=grow, eg=eg, glast=glast,
                           kd=kn * jnp.exp(glast - gcum), vb=xv * bexp, kb=kn * bexp * eg))

    decays, a_list = [], []
    for ck in chunks:
        for h in range(N_HEADS):
            rel = ck["gcum"][:, HEAD_DIM * h:HEAD_DIM * h + C] - ck["grow"][4 + h:5 + h, :]
            decay = jnp.where(incl, jnp.exp(jnp.where(incl, rel, 0.0)), 0.0)
            kk = _nt_b(jnp.where(heads[h], ck["kn"], 0.0), ck["kn"])
            decays.append(decay)
            a_list.append(jnp.where(strict, ck["bexp"][:, HEAD_DIM * h:HEAD_DIM * h + C] * kk * decay, 0.0))
    t_list = _inv_unit_lower(a_list, C)
    for c, ck in enumerate(chunks):
        rhs = jnp.concatenate([ck["vb"], ck["kb"]], axis=1)
        sol = [_mm_b(t_list[N_HEADS * c + h], rhs) for h in range(N_HEADS)]
        usel = sol[0][:, 0:W_MIX]
        wsel = sol[0][:, W_MIX:2 * W_MIX]
        for h in range(1, N_HEADS):
            usel = jnp.where(heads[h], sol[h][:, 0:W_MIX], usel)
            wsel = jnp.where(heads[h], sol[h][:, W_MIX:2 * W_MIX], wsel)
        ck["usel"], ck["wsel"] = usel, wsel
        ck["qks"] = [_nt_b(jnp.where(heads[h], ck["qn"], 0.0), ck["kn"]) * decays[N_HEADS * c + h]
                     for h in range(N_HEADS)]

    s = sbd[...]
    for c, ck in enumerate(chunks):
        u = ck["usel"] - _mm_b(ck["wsel"], s)
        o = _mm_b(ck["qn"] * ck["eg"], s)
        for h in range(N_HEADS):
            o = o + jnp.where(heads[h], _mm_b(ck["qks"][h], u), 0.0)
        upd = lax.dot_general(ck["kd"].astype(BF16), u.astype(BF16), (((0,), (0,)), ((), ())),
                              preferred_element_type=F32)
        s = s * jnp.exp(ck["glast"]) + jnp.where(bdmask, upd, 0.0)
        o_ref[c * C:(c + 1) * C, :] = o
    sbd[...] = s

    @pl.when(step == pl.num_programs(1) - 1)
    def _():
        sout_ref[...] = sbd[...]


def _gdn(qkv, prev8, small, smallt, convw, s0, C, CPS, nvalid):
    nseq, L, _ = qkv.shape
    R = C * CPS
    nsteps = L // R
    hidx = np.arange(W_MIX) // HEAD_DIM
    bd = jnp.asarray((hidx[:, None] == hidx[None, :]).astype(np.float32))
    eg = np.zeros((LANES, W_MIX), np.float32)
    eb = np.zeros((LANES, W_MIX), np.float32)
    eg[4 + hidx, np.arange(W_MIX)] = 1.0
    eb[8 + hidx, np.arange(W_MIX)] = 1.0
    const = lambda b, i: (0, 0)
    return pl.pallas_call(
        functools.partial(_gdn_kernel, C=C, CPS=CPS, nvalid=nvalid),
        grid=(nseq, nsteps),
        in_specs=[
            pl.BlockSpec((None, R, 3 * W_MIX), lambda b, i: (b, i, 0)),
            pl.BlockSpec((None, 8, 3 * W_MIX), lambda b, i: (b, 0, 0)),
            pl.BlockSpec((None, R, LANES), lambda b, i: (b, i, 0)),
            pl.BlockSpec((None, 16, max(R, LANES)), lambda b, i: (b, 0, i)),
            pl.BlockSpec((8, 3 * W_MIX), const),
            pl.BlockSpec((None, W_MIX, W_MIX), lambda b, i: (b, 0, 0)),
            pl.BlockSpec((W_MIX, W_MIX), const),
            pl.BlockSpec((LANES, W_MIX), const),
            pl.BlockSpec((LANES, W_MIX), const),
        ],
        out_specs=[
            pl.BlockSpec((None, R, W_MIX), lambda b, i: (b, i, 0)),
            pl.BlockSpec((None, W_MIX, W_MIX), lambda b, i: (b, 0, 0)),
        ],
        out_shape=[jax.ShapeDtypeStruct((nseq, L, W_MIX), F32),
                   jax.ShapeDtypeStruct((nseq, W_MIX, W_MIX), F32)],
        scratch_shapes=[pltpu.VMEM((8 + R, 3 * W_MIX), F32), pltpu.VMEM((W_MIX, W_MIX), F32)],
        compiler_params=pltpu.CompilerParams(dimension_semantics=("arbitrary", "arbitrary"),
                                             vmem_limit_bytes=VMEM_LIMIT),
        name="gdn",
    )(qkv, prev8, small, smallt, convw, s0, bd, jnp.asarray(eg), jnp.asarray(eb))


def _ret_kernel(q_ref, k_ref, v_ref, dmat_ref, qdec_ref, kdec_ref, cdec_ref, bd_ref, s0_ref,
                o_ref, sout_ref, sbd):
    step = pl.program_id(1)

    @pl.when(step == 0)
    def _():
        sbd[...] = s0_ref[...]

    q = q_ref[...]
    k = k_ref[...]
    v = v_ref[...]
    s = sbd[...]
    lq = lax.broadcasted_iota(jnp.int32, (1, LANES), 1)
    lv = lax.broadcasted_iota(jnp.int32, (1, W_MIX), 1)
    o = _mm_b(q * qdec_ref[...], s)
    for h in range(N_HEADS):
        qk = _nt_b(jnp.where((lq // RET_QK) == h, q, 0.0), k) * dmat_ref[h]
        o = o + jnp.where((lv // HEAD_DIM) == h, _mm_b(qk, v), 0.0)
    upd = lax.dot_general((k * kdec_ref[...]).astype(BF16), v.astype(BF16), (((0,), (0,)), ((), ())),
                          preferred_element_type=F32)
    sbd[...] = s * cdec_ref[...] + jnp.where(bd_ref[...] > 0.5, upd, 0.0)
    o_ref[...] = o

    @pl.when(step == pl.num_programs(1) - 1)
    def _():
        sout_ref[...] = sbd[...]


def _ret_consts(C, nvalid):
    lg = np.log(1.0 - np.exp2(-5.0 - np.arange(N_HEADS, dtype=np.float64)))
    idx = np.arange(C, dtype=np.float64)
    rel = idx[:, None] - idx[None, :]
    dmat = np.where(rel >= 0, np.exp(np.maximum(rel, 0.0) * lg[:, None, None]), 0.0)
    qd = np.exp((idx + 1.0)[None, :] * lg[:, None])
    kd = np.where(idx[None, :] < nvalid, np.exp((nvalid - 1.0 - idx)[None, :] * lg[:, None]), 0.0)
    cd = np.exp(nvalid * lg)
    qdec = np.repeat(qd.T, RET_QK, axis=1)
    kdec = np.repeat(kd.T, RET_QK, axis=1)
    cdec = np.repeat(cd, HEAD_DIM)[None, :]
    bd = (np.arange(LANES)[:, None] // RET_QK) == (np.arange(W_MIX)[None, :] // HEAD_DIM)
    f = lambda a: jnp.asarray(a.astype(np.float32))
    return f(dmat), f(qdec), f(kdec), f(cdec), f(bd)


def _ret(q, k, v, s0, C, nvalid):
    nseq, L, _ = q.shape
    dmat, qdec, kdec, cdec, bd = _ret_consts(C, nvalid)
    const2 = lambda b, i: (0, 0)
    return pl.pallas_call(
        _ret_kernel,
        grid=(nseq, L // C),
        in_specs=[
            pl.BlockSpec((None, C, LANES), lambda b, i: (b, i, 0)),
            pl.BlockSpec((None, C, LANES), lambda b, i: (b, i, 0)),
            pl.BlockSpec((None, C, W_MIX), lambda b, i: (b, i, 0)),
            pl.BlockSpec((N_HEADS, C, C), lambda b, i: (0, 0, 0)),
            pl.BlockSpec((C, LANES), const2),
            pl.BlockSpec((C, LANES), const2),
            pl.BlockSpec((1, W_MIX), const2),
            pl.BlockSpec((LANES, W_MIX), const2),
            pl.BlockSpec((None, LANES, W_MIX), lambda b, i: (b, 0, 0)),
        ],
        out_specs=[
            pl.BlockSpec((None, C, W_MIX), lambda b, i: (b, i, 0)),
            pl.BlockSpec((None, LANES, W_MIX), lambda b, i: (b, 0, 0)),
        ],
        out_shape=[jax.ShapeDtypeStruct((nseq, L, W_MIX), F32),
                   jax.ShapeDtypeStruct((nseq, LANES, W_MIX), F32)],
        scratch_shapes=[pltpu.VMEM((LANES, W_MIX), F32)],
        compiler_params=pltpu.CompilerParams(dimension_semantics=("arbitrary", "arbitrary"),
                                             vmem_limit_bytes=VMEM_LIMIT),
        name="retention",
    )(q, k, v, dmat, qdec, kdec, cdec, bd, s0)


def _silu(x):
    return x * (1.0 / (1.0 + jnp.exp(-x)))


def _post_kernel(oa_ref, ob_ref, oc_ref, od_ref, gz_ref, rz_ref, x_ref, gains_ref, bd_ref, wout_ref,
                 gffn_ref, wr_ref, br_ref, y_ref, h_ref, gate_ref, *, lam_init):
    bd16 = bd_ref[...].astype(BF16)

    def hnorm(o, gain):
        ms = _mm01_rows(o * o, bd16) * (1.0 / HEAD_DIM)
        return o * lax.rsqrt(ms + RMS_EPS) * gain

    out_a = hnorm(oa_ref[...], gains_ref[0:1, :])
    out_b = hnorm(ob_ref[...], gains_ref[1:2, :]) * _silu(gz_ref[...])
    out_c = hnorm(oc_ref[...], gains_ref[2:3, :]) * _silu(rz_ref[...])
    out_d = hnorm(od_ref[...], gains_ref[3:4, :]) * (1.0 - lam_init)
    y = x_ref[...]
    for k, part in enumerate((out_a, out_b, out_c, out_d)):
        y = y + _mm(part.astype(BF16), wout_ref[k * W_MIX:(k + 1) * W_MIX, :])
    y_ref[...] = y

    ms = jnp.mean(y * y, axis=-1, keepdims=True)
    h = y * lax.rsqrt(ms + RMS_EPS) * gffn_ref[...]
    h_ref[...] = h.astype(BF16)

    lg = _mm(h, wr_ref[...], HI) + br_ref[...]
    lane = lax.broadcasted_iota(jnp.int32, lg.shape, 1)
    lanef = lane.astype(F32)
    is_g = lane < N_GROUPS
    gmax = jnp.max(jnp.where(is_g, lg, NEG), axis=-1, keepdims=True)
    grp = jnp.min(jnp.where(is_g & (lg == gmax), lanef, float(LANES)), axis=-1, keepdims=True).astype(jnp.int32)
    p_sel = 1.0 / jnp.sum(jnp.where(is_g, jnp.exp(lg - gmax), 0.0), axis=-1, keepdims=True)
    e_lane = lane - N_GROUPS
    in_grp = (e_lane >= 0) & (e_lane < N_EXPERTS) & ((e_lane // E_PER_GROUP) == grp)
    vals = jnp.where(in_grp, lg, NEG)
    v1 = jnp.max(vals, axis=-1, keepdims=True)
    i1 = jnp.min(jnp.where(in_grp & (vals == v1), lanef, float(LANES)), axis=-1, keepdims=True).astype(jnp.int32)
    vals2 = jnp.where(lane == i1, NEG, vals)
    v2 = jnp.max(vals2, axis=-1, keepdims=True)
    i2 = jnp.min(jnp.where(in_grp & (lane != i1) & (vals2 == v2), lanef, float(LANES)), axis=-1,
                 keepdims=True).astype(jnp.int32)
    e2 = jnp.exp(v2 - v1)
    w1 = p_sel / (1.0 + e2)
    w2 = p_sel * e2 / (1.0 + e2)
    gate_ref[...] = jnp.where(lane == i1, w1, 0.0) + jnp.where(lane == i2, w2, 0.0)


def _post(oa, ob, oc, od, gz, rz, x2d, gains, wout, gffn, wr, br, lam_init, tm):
    rows = x2d.shape[0]
    hidx = np.arange(W_MIX) // HEAD_DIM
    bd = jnp.asarray((hidx[:, None] == hidx[None, :]).astype(np.float32))
    rowb = lambda w: pl.BlockSpec((tm, w), lambda i: (i, 0))
    const = lambda i: (0, 0)
    return pl.pallas_call(
        functools.partial(_post_kernel, lam_init=lam_init),
        grid=(rows // tm,),
        in_specs=[rowb(W_MIX)] * 6 + [
            rowb(D_MODEL),
            pl.BlockSpec((8, W_MIX), const),
            pl.BlockSpec((W_MIX, W_MIX), const),
            pl.BlockSpec((D_MODEL, D_MODEL), const),
            pl.BlockSpec((1, D_MODEL), const),
            pl.BlockSpec((D_MODEL, LANES), const),
            pl.BlockSpec((1, LANES), const),
        ],
        out_specs=[rowb(D_MODEL), rowb(D_MODEL), rowb(LANES)],
        out_shape=[jax.ShapeDtypeStruct((rows, D_MODEL), F32),
                   jax.ShapeDtypeStruct((rows, D_MODEL), BF16),
                   jax.ShapeDtypeStruct((rows, LANES), F32)],
        compiler_params=pltpu.CompilerParams(dimension_semantics=("arbitrary",),
                                             vmem_limit_bytes=VMEM_LIMIT),
        name="post_mix",
    )(oa, ob, oc, od, gz, rz, x2d, gains, bd, wout, gffn, wr, br)


def _moe_kernel(h_ref, gate_ref, y_ref, wg_ref, wu_ref, wd_ref, gfin_ref, o_ref, *, final_norm):
    e = pl.program_id(1)

    @pl.when(e == 0)
    def _():
        o_ref[...] = y_ref[...]

    h = h_ref[...]
    a = _mm(h, wg_ref[...])
    u = _mm(h, wu_ref[...])
    gates = gate_ref[...]
    lane = lax.broadcasted_iota(jnp.int32, gates.shape, 1)
    ge = jnp.sum(jnp.where(lane == e + N_GROUPS, gates, 0.0), axis=-1, keepdims=True)
    act = _silu(a) * u * ge
    o_ref[...] += _mm(act.astype(BF16), wd_ref[...])

    if final_norm:
        @pl.when(e == N_EXPERTS - 1)
        def _():
            y = o_ref[...]
            ms = jnp.mean(y * y, axis=-1, keepdims=True)
            o_ref[...] = y * lax.rsqrt(ms + RMS_EPS) * gfin_ref[...]


def _moe(h, gates, y, wg, wu, wd, gfin, final_norm, tm):
    rows = h.shape[0]
    return pl.pallas_call(
        functools.partial(_moe_kernel, final_norm=final_norm),
        grid=(rows // tm, N_EXPERTS),
        in_specs=[
            pl.BlockSpec((tm, D_MODEL), lambda i, e: (i, 0)),
            pl.BlockSpec((tm, LANES), lambda i, e: (i, 0)),
            pl.BlockSpec((tm, D_MODEL), lambda i, e: (i, 0)),
            pl.BlockSpec((None, D_MODEL, D_FF), lambda i, e: (e, 0, 0)),
            pl.BlockSpec((None, D_MODEL, D_FF), lambda i, e: (e, 0, 0)),
            pl.BlockSpec((None, D_FF, D_MODEL), lambda i, e: (e, 0, 0)),
            pl.BlockSpec((1, D_MODEL), lambda i, e: (0, 0)),
        ],
        out_specs=pl.BlockSpec((tm, D_MODEL), lambda i, e: (i, 0)),
        out_shape=jax.ShapeDtypeStruct((rows, D_MODEL), F32),
        compiler_params=pltpu.CompilerParams(dimension_semantics=("arbitrary", "arbitrary"),
                                             vmem_limit_bytes=VMEM_LIMIT),
        name="experts",
    )(h, gates, y, wg, wu, wd, gfin)


_SPLIT = (256, 256, 256, 4, 256, 256, 256, 4, 4, 256, 128, 128, 256, 256, 256, 256, 256)


def _arrange_w_in(w):
    idx = np.cumsum(_SPLIT)[:-1].tolist()
    (fq, fk, fv, ff, gq, gk, gv, ga, gb, gz, rq, rk, rv, rz, dq, dk, dv) = jnp.split(w, idx, axis=-1)
    small = jnp.concatenate([ff, ga, gb, jnp.zeros((D_MODEL, LANES - 12), w.dtype)], axis=-1)
    big = jnp.concatenate([fq, fk, fv, gq, gk, gv, gz, rq, rk, rv, rz, dq, dk, dv, small], axis=-1)
    return big.astype(BF16), small[:, :16].T.astype(BF16)


def _pad_lanes(v, n=LANES):
    v = v.reshape(1, -1).astype(F32)
    return jnp.pad(v, ((0, 0), (0, n - v.shape[1])))


def _block_diag_state(s):
    n, hh, dk, dv = s.shape
    eye = jnp.eye(hh, dtype=s.dtype)
    return jnp.einsum('nhkv,hg->nhkgv', s, eye).reshape(n, hh * dk, hh * dv)


def _diag_blocks(sbd, dk, dv):
    n = sbd.shape[0]
    s5 = sbd.reshape(n, N_HEADS, dk, N_HEADS, dv)
    return jnp.stack([s5[:, h, :, h, :] for h in range(N_HEADS)], axis=1)


def _largest_tile(rows, cap):
    t = cap
    while rows % t:
        t //= 2
    return t


def kernel(x_prompt, x_sample, cache_fox_k, cache_fox_v, cache_fox_logf, cache_diff_k, cache_diff_v,
           state_gdn, state_gdn_conv, state_ret, page_table,
           g_mix_norm, w_in, b_fox_f, g_fox_out, w_gdn_conv, gdn_a_log, gdn_dt_bias, g_gdn_out,
           g_ret_out, diff_lambda_q1, diff_lambda_k1, diff_lambda_q2, diff_lambda_k2, g_diff_out,
           w_out, g_ffn_norm, w_router_group, b_router_group, w_router_expert, b_router_expert,
           w_exp_gate, w_exp_up, w_exp_down, g_final):
    depth = w_in.shape[0]
    _, S, _ = x_prompt.shape
    N, L, _ = x_sample.shape
    n_pool = cache_fox_k.shape[1]
    assert x_prompt.shape[0] == 1 and L == 4 and S % 1024 == 0

    yp = x_prompt.reshape(S, D_MODEL)
    ys = x_sample.reshape(N * L, D_MODEL)
    slopes = jnp.exp2(-8.0 * jnp.arange(1, N_HEADS + 1, dtype=F32) / N_HEADS)
    gfin = g_final.reshape(1, D_MODEL)
    TQ_ATT, TK_ATT = 1024, 512
    tm_p = _largest_tile(S, 512)
    tm_moe = _largest_tile(S, 1024)

    def slab_view(c):
        return c.transpose(0, 1, 3, 4, 2).reshape(depth, n_pool, W_MIX, PAGE)

    fk_t, fv_t = slab_view(cache_fox_k), slab_view(cache_fox_v)
    dk_t, dv_t = slab_view(cache_diff_k), slab_view(cache_diff_v)

    ent_p, ent_s = [], []
    for l in range(depth):
        lam_init = 0.8 - 0.6 * math.exp(-0.3 * l)
        last = l == depth - 1
        w_arr, wst = _arrange_w_in(w_in[l])
        brow = _pad_lanes(jnp.concatenate([b_fox_f[l], gdn_dt_bias[l]]))
        arow = _pad_lanes(jnp.concatenate([jnp.zeros((4,), F32), gdn_a_log[l]]))
        bcol = jnp.broadcast_to(brow[0, :16, None], (16, LANES))
        acol = jnp.broadcast_to(arow[0, :16, None], (16, LANES))
        gmix = g_mix_norm[l].reshape(1, D_MODEL)
        lam_params = jnp.pad(jnp.stack([diff_lambda_q1[l], diff_lambda_k1[l],
                                        diff_lambda_q2[l], diff_lambda_k2[l]]).astype(F32),
                             ((0, 4), (0, LANES - DIFF_QK)))
        convw = jnp.pad(w_gdn_conv[l].astype(F32), ((0, 8 - GDN_CONV), (0, 0)))
        gains = jnp.pad(jnp.stack([g_fox_out[l].reshape(-1), jnp.tile(g_gdn_out[l], N_HEADS),
                                   g_ret_out[l].reshape(-1), jnp.tile(g_diff_out[l], N_HEADS)]).astype(F32),
                        ((0, 4), (0, 0)))
        wout = w_out[l].astype(BF16)
        gffn = g_ffn_norm[l].reshape(1, D_MODEL)
        wr = jnp.pad(jnp.concatenate([w_router_group[l], w_router_expert[l].reshape(D_MODEL, N_EXPERTS)],
                                     axis=-1).astype(F32), ((0, 0), (0, LANES - N_GROUPS - N_EXPERTS)))
        br = _pad_lanes(jnp.concatenate([b_router_group[l], b_router_expert[l].reshape(-1)]))
        wg = w_exp_gate[l].astype(BF16)
        wu = w_exp_up[l].astype(BF16)
        wd = w_exp_down[l].astype(BF16)

        zp = _inproj(yp, gmix, w_arr, wst, brow, arow, bcol, acol, tm_p)
        o_a = _fox_prompt(zp[I_FQ], zp[I_FKB], zp[I_FVB], zp[I_CT][0:4], TQ_ATT, TK_ATT)
        o_d = _diff_prompt(zp[I_DQ], zp[I_DKB], zp[I_DVB], slopes, lam_params, lam_init,
                           TQ_ATT, TK_ATT)
        o_b, sb_p = _gdn(zp[I_GQKV][None], jnp.zeros((1, 8, 3 * W_MIX), F32), zp[I_SMALL][None],
                         zp[I_SMALLT][None], convw, jnp.zeros((1, W_MIX, W_MIX), F32),
                         GDN_CHUNK, 4, GDN_CHUNK)
        o_c, sc_p = _ret(zp[I_RQ][None], zp[I_RK][None], zp[I_RV][None],
                         jnp.zeros((1, LANES, W_MIX), F32), RET_CHUNK, RET_CHUNK)
        y1, h2, gates = _post(o_a, o_b[0], o_c[0], o_d, zp[I_GZ], zp[I_RZ], yp, gains, wout, gffn,
                              wr, br, lam_init, tm_p)
        yp = _moe(h2, gates, y1, wg, wu, wd, gfin, last, tm_moe)
        ent_p.append((
            zp[I_FK].reshape(1, S, N_HEADS, HEAD_DIM), zp[I_FV].reshape(1, S, N_HEADS, HEAD_DIM),
            zp[I_SMALLT][0:4].T.reshape(1, S, N_HEADS),
            zp[I_DK].reshape(1, S, N_HEADS, HEAD_DIM), zp[I_DV].reshape(1, S, N_HEADS, HEAD_DIM),
            _diag_blocks(sb_p, HEAD_DIM, HEAD_DIM),
            zp[I_GQKV][S - (GDN_CONV - 1):].reshape(1, GDN_CONV - 1, 3 * W_MIX),
            _diag_blocks(sc_p, RET_QK, HEAD_DIM)))

        R = N * L
        zs = _inproj(ys, gmix, w_arr, wst, brow, arow, bcol, acol, R)

        def seq8(a):
            return jnp.pad(a.astype(F32).reshape(N, L, -1), ((0, 0), (0, 8 - L), (0, 0)))

        def seq_page_t(a):
            return jnp.pad(a.reshape(N, L, -1), ((0, 0), (0, PAGE - L), (0, 0))).transpose(0, 2, 1)

        logf_new = zs[I_SMALLT][0:4].reshape(4, N, L).transpose(1, 0, 2)
        flogn = jnp.pad(logf_new, ((0, 0), (0, 4), (0, PAGE - L)))
        o_af, o_df = _decode(
            page_table, l, seq8(zs[I_FQ]), seq8(zs[I_DQ]), fk_t, fv_t,
            jnp.pad(cache_fox_logf[l].transpose(0, 2, 1), ((0, 0), (0, 4), (0, 0))), dk_t, dv_t,
            seq_page_t(zs[I_FKB]), seq_page_t(zs[I_FVB]), flogn, seq_page_t(zs[I_DKB]),
            seq_page_t(zs[I_DVB]), lam_params, lam_init)
        o_a_s = o_af[:, :L].reshape(R, W_MIX)
        o_d_s = o_df[:, :L].reshape(R, W_MIX)

        def seq_rows(a, rows):
            return jnp.pad(a.reshape(N, L, -1), ((0, 0), (0, rows - L), (0, 0)))

        prev8 = jnp.pad(state_gdn_conv[l].astype(F32), ((0, 0), (8 - (GDN_CONV - 1), 0), (0, 0)))
        smallt_s = jnp.pad(zs[I_SMALLT].reshape(16, N, L).transpose(1, 0, 2),
                           ((0, 0), (0, 0), (0, LANES - L)))
        o_bs, sb_s = _gdn(seq_rows(zs[I_GQKV], GDN_CHUNK), prev8, seq_rows(zs[I_SMALL], GDN_CHUNK),
                          smallt_s, convw, _block_diag_state(state_gdn[l].astype(F32)),
                          GDN_CHUNK, 1, L)
        o_cs, sc_s = _ret(seq_rows(zs[I_RQ], RET_CHUNK), seq_rows(zs[I_RK], RET_CHUNK),
                          seq_rows(zs[I_RV], RET_CHUNK), _block_diag_state(state_ret[l].astype(F32)),
                          RET_CHUNK, L)
        o_b_s = o_bs[:, :L].reshape(R, W_MIX)
        o_c_s = o_cs[:, :L].reshape(R, W_MIX)
        y1s, h2s, gates_s = _post(o_a_s, o_b_s, o_c_s, o_d_s, zs[I_GZ], zs[I_RZ], ys, gains, wout,
                                  gffn, wr, br, lam_init, R)
        ys = _moe(h2s, gates_s, y1s, wg, wu, wd, gfin, last, R)
        gq = zs[I_GQKV].reshape(N, L, 3 * W_MIX)
        conv_new = jnp.concatenate([state_gdn_conv[l].astype(F32), gq], axis=1)[:, L:]
        ent_s.append((
            zs[I_FK].reshape(N, L, N_HEADS, HEAD_DIM), zs[I_FV].reshape(N, L, N_HEADS, HEAD_DIM),
            logf_new.transpose(0, 2, 1),
            zs[I_DK].reshape(N, L, N_HEADS, HEAD_DIM), zs[I_DV].reshape(N, L, N_HEADS, HEAD_DIM),
            _diag_blocks(sb_s, HEAD_DIM, HEAD_DIM), conv_new, _diag_blocks(sc_s, RET_QK, HEAD_DIM)))

    stack = lambda ents, i: jnp.stack([e[i] for e in ents])
    return ((yp.reshape(1, S, D_MODEL), ys.reshape(N, L, D_MODEL))
            + tuple(stack(ent_p, i) for i in range(8))
            + tuple(stack(ent_s, i) for i in range(8)))
```
